```python
import math
import jax, jax.numpy as jnp
from jax import lax
import numpy as np

D_MODEL = 4096
BATCH = 1
SEQ = 8192
DEPTH = 4

CTX_LEN = 256
GRID_W = 64

HY_WIDTH = D_MODEL // 2
HY_ORDER = 2
HY_SHORT = 3
HY_EMB = 33
HY_BANDS = (HY_EMB - 1) // 2
HY_FILTER_FF = 64
HY_TARGET = 1e-2
HY_FAST_DECAY_PCT = 0.3
HY_SLOW_DECAY_PCT = 1.5
HY_MAX_DECAY = math.log(HY_TARGET) / HY_FAST_DECAY_PCT
HY_MIN_DECAY = math.log(HY_TARGET) / HY_SLOW_DECAY_PCT
DN_HEADS = 16
DN_HEAD_DIM = 128
DN_WIDTH = DN_HEADS * DN_HEAD_DIM
DN_SHORT = 5
DN_CHUNK = 64
EVEN_IN = 3 * HY_WIDTH + 4 * DN_WIDTH + 4 * DN_HEADS
EVEN_MIX = HY_WIDTH + DN_WIDTH
S5_GROUP = 16
S5_GROUPS = D_MODEL // S5_GROUP
S5_STATE = 64
S5_BLOCK = 32
S5_MAX_RE = -1e-4
N_EXPERTS = 16
EXPERT_FF = 384
EC_CAPACITY = 2
DEEPNORM_ALPHA = (2 * DEPTH) ** 0.25
DEEPNORM_BETA = (8 * DEPTH) ** -0.25
LN_EPS = 1e-5
N_EVEN = (DEPTH + 1) // 2
N_ODD = DEPTH // 2

kernel_name = 'hybrid_hyena_deltanet_s5_ecmoe_dit'


def post_norm(h, y, g, b):
    z = (DEEPNORM_ALPHA * h + y).astype(jnp.float32)
    mu = jnp.mean(z, axis=-1, keepdims=True)
    var = jnp.mean(jnp.square(z - mu), axis=-1, keepdims=True)
    return ((z - mu) * lax.rsqrt(var + LN_EPS) * g.astype(jnp.float32) + b.astype(jnp.float32)).astype(h.dtype)


def short_conv(u, w):
    k, ch = w.shape
    return lax.conv_general_dilated(u, w.astype(u.dtype)[:, None, :], window_strides=(1,), padding=[(k // 2, k // 2)], dimension_numbers=('NWC', 'WIO', 'NWC'), feature_group_count=ch)


def cat_streams(a_ctx, a_lat, rev):
    if rev:
        return jnp.concatenate([a_ctx[:, ::-1], a_lat[:, ::-1]], axis=1)
    return jnp.concatenate([a_ctx, a_lat], axis=1)


def to_col_major(h):
    b_, n, d = h.shape
    rows = n // GRID_W
    return h.reshape(b_, rows, GRID_W, d).transpose(0, 2, 1, 3).reshape(b_, n, d)


def from_col_major(h):
    b_, n, d = h.shape
    rows = n // GRID_W
    return h.reshape(b_, GRID_W, rows, d).transpose(0, 2, 1, 3).reshape(b_, n, d)


def hyena_filters(L, w1, b1, w2, b2, w3, b3, w4, freq):
    f32 = jnp.float32
    pos = jnp.arange(L, dtype=f32)
    t = pos / max(L - 1, 1)
    ang = (2.0 * math.pi / L) * pos
    bands = jnp.linspace(1e-4, HY_BANDS - 1, HY_BANDS, dtype=f32)
    z = jnp.concatenate([t[:, None], jnp.cos(ang[:, None] * bands), -jnp.sin(ang[:, None] * bands)], axis=-1)
    freq = freq.astype(f32)
    h = jnp.sin(freq[0] * (z @ w1.astype(f32) + b1.astype(f32)))
    h = jnp.sin(freq[1] * (h @ w2.astype(f32) + b2.astype(f32)))
    h = jnp.sin(freq[2] * (h @ w3.astype(f32) + b3.astype(f32)))
    h = (h @ w4.astype(f32)).reshape(L, HY_ORDER, 2, HY_WIDTH)
    deltas = jnp.abs(jnp.linspace(HY_MIN_DECAY, HY_MAX_DECAY, HY_WIDTH, dtype=f32))
    h = h * jnp.exp(-t[:, None, None, None] * deltas)
    fwd, bwd = h[:, :, 0], h[:, :, 1]
    return jnp.concatenate([fwd, jnp.zeros_like(fwd[:1]), bwd[:0:-1]], axis=0)


def long_conv(u, filt, bias):
    L = u.shape[1]
    uf = jnp.fft.rfft(u.astype(jnp.float32), n=2 * L, axis=1)
    hf = jnp.fft.rfft(filt, axis=0)
    y = jnp.fft.irfft(uf * hf, n=2 * L, axis=1)[:, :L]
    return (y + u * bias).astype(u.dtype)


def hyena(hy, conv_w, filt_params, bias):
    L = hy.shape[1]
    v, x1, x2 = jnp.split(short_conv(hy, conv_w), 3, axis=-1)
    filt = hyena_filters(L, *filt_params)
    bias = bias.astype(jnp.float32)
    z = x1 * long_conv(v, filt[:, 0], bias[0])
    return x2 * long_conv(z, filt[:, 1], bias[1])


def l2norm(a):
    return a * lax.rsqrt(jnp.sum(a * a, axis=-1, keepdims=True) + 1e-6)


def gated_delta_chunked(q, k, v, g, beta):
    b_, t, h, dk = q.shape
    dv = v.shape[-1]
    n = t // DN_CHUNK

    def blocks(a):
        a = a.astype(jnp.float32).reshape((b_, n, DN_CHUNK, h) + a.shape[3:])
        return jnp.moveaxis(a, (1, 3), (0, 2))

    qc, kc, vc, bc = blocks(q), blocks(k), blocks(v), blocks(beta)
    gc = jnp.cumsum(blocks(g), axis=-1)
    causal = jnp.tril(jnp.ones((DN_CHUNK, DN_CHUNK), bool))
    strict = jnp.tril(jnp.ones((DN_CHUNK, DN_CHUNK), bool), -1)
    decay = jnp.exp(jnp.where(causal, gc[..., :, None] - gc[..., None, :], -jnp.inf))
    kb = kc * bc[..., None]
    a_mat = jnp.where(strict, jnp.einsum('nbhid,nbhjd->nbhij', kb, kc) * decay, 0.0)
    t_mat = a_mat + jnp.eye(DN_CHUNK, dtype=jnp.float32)
    u = lax.linalg.triangular_solve(t_mat, vc * bc[..., None], left_side=True, lower=True, unit_diagonal=True)
    w = lax.linalg.triangular_solve(t_mat, kb * jnp.exp(gc)[..., None], left_side=True, lower=True, unit_diagonal=True)
    qk = jnp.einsum('nbhid,nbhjd->nbhij', qc, kc) * decay

    def step(state, xs):
        q_i, k_i, u_i, w_i, g_i, qk_i = xs
        v_new = u_i - jnp.einsum('bhcd,bhde->bhce', w_i, state)
        o = jnp.einsum('bhcd,bhde->bhce', q_i * jnp.exp(g_i)[..., None], state) + jnp.einsum('bhij,bhje->bhie', qk_i, v_new)
        g_last = g_i[..., -1:]
        k_dec = k_i * jnp.exp(g_last - g_i)[..., None]
        state = state * jnp.exp(g_last)[..., None] + jnp.einsum('bhcd,bhce->bhde', k_dec, v_new)
        return state, o

    s0 = jnp.zeros((b_, h, dk, dv), jnp.float32)
    _, o = lax.scan(step, s0, (qc, kc, u, w, gc, qk))
    return jnp.moveaxis(o, (0, 2), (1, 3)).reshape(b_, t, h, dv)


def gated_head_norm(o, gate, w):
    b_, t_ = gate.shape[:2]
    o = o * lax.rsqrt(jnp.mean(o * o, axis=-1, keepdims=True) + 1e-6) * w.astype(jnp.float32)
    o = o * jax.nn.silu(gate.astype(jnp.float32)).reshape(b_, t_, DN_HEADS, DN_HEAD_DIM)
    return o.reshape(b_, t_, DN_WIDTH).astype(gate.dtype)


def deltanet(dn_ctx, dn_lat, ctx_out, conv_w, a_log, dt_bias, norm_w):
    a_rate = jnp.exp(a_log.astype(jnp.float32))
    dt_bias = dt_bias.astype(jnp.float32)

    def prep(dn):
        b_, t_, _ = dn.shape
        qkv = jax.nn.silu(short_conv(dn[..., :3 * DN_WIDTH], conv_w)).astype(jnp.float32)
        q, k, v = (a.reshape(b_, t_, DN_HEADS, DN_HEAD_DIM) for a in jnp.split(qkv, 3, axis=-1))
        ab = dn[..., 4 * DN_WIDTH:].astype(jnp.float32).reshape(b_, t_, 2, 2, DN_HEADS)
        beta = jax.nn.sigmoid(ab[:, :, 0])
        g = -a_rate * jax.nn.softplus(ab[:, :, 1] + dt_bias)
        return l2norm(q) * DN_HEAD_DIM ** -0.5, l2norm(k), v, beta, g

    pc, pl = prep(dn_ctx), prep(dn_lat)
    tc = dn_ctx.shape[1]
    o_ctx, o_lat = 0.0, 0.0
    for direction in (0, 1):
        rev = direction == 1
        o = gated_delta_chunked(cat_streams(pc[0], pl[0], rev), cat_streams(pc[1], pl[1], rev), cat_streams(pc[2], pl[2], rev), cat_streams(pc[4][:, :, direction], pl[4][:, :, direction], rev), cat_streams(pc[3][:, :, direction], pl[3][:, :, direction], rev))
        oc, ol = o[:, :tc], o[:, tc:]
        if rev:
            oc, ol = oc[:, ::-1], ol[:, ::-1]
        o_ctx = o_ctx + oc
        o_lat = o_lat + ol
    y_lat = gated_head_norm(o_lat, dn_lat[..., 3 * DN_WIDTH:4 * DN_WIDTH], norm_w)
    y_ctx = gated_head_norm(o_ctx, dn_ctx[..., 3 * DN_WIDTH:4 * DN_WIDTH], norm_w) if ctx_out else None
    return y_ctx, y_lat


def even_mixer(u_ctx, u_lat, ctx_out, w_in, hy_conv, hy_w1, hy_b1, hy_w2, hy_b2, hy_w3, hy_b3, hy_w4, hy_freq, hy_bias, dn_conv, dn_a_log, dn_dt_bias, dn_norm_w):
    hy_cols = 3 * HY_WIDTH
    filt = (hy_w1, hy_b1, hy_w2, hy_b2, hy_w3, hy_b3, hy_w4, hy_freq)
    z_lat = u_lat @ w_in
    z_ctx = u_ctx @ (w_in if ctx_out else w_in[:, hy_cols:])
    dn_ctx = z_ctx[..., hy_cols:] if ctx_out else z_ctx
    y_dn_ctx, y_dn_lat = deltanet(dn_ctx, z_lat[..., hy_cols:], ctx_out, dn_conv, dn_a_log, dn_dt_bias, dn_norm_w)
    y_lat = jnp.concatenate([hyena(z_lat[..., :hy_cols], hy_conv, filt, hy_bias), y_dn_lat], axis=-1)
    y_ctx = jnp.concatenate([hyena(z_ctx[..., :hy_cols], hy_conv, filt, hy_bias), y_dn_ctx], axis=-1) if ctx_out else None
    return y_ctx, y_lat


def s5_combine(left, right):
    a_l, b_l = left
    a_r, b_r = right
    return a_l * a_r, a_r * b_l + b_r


def s5_block(args):
    u, lam_re, lam_im, log_dt, b_re, b_im, c_re, c_im = args
    lam = lax.complex(jnp.minimum(lam_re, S5_MAX_RE), lam_im)
    lam_bar = jnp.exp(lam * jnp.exp(log_dt)[..., None])
    b_bar = ((lam_bar - 1.0) / lam)[..., None] * lax.complex(b_re, b_im)
    bu = jnp.einsum('dgpc,dbtgc->dbtgp', b_bar, u.astype(jnp.complex64))
    a = jnp.broadcast_to(lam_bar[:, None, None], bu.shape)
    _, states = lax.associative_scan(s5_combine, (a, bu), axis=2)
    return jnp.einsum('dgcp,dbtgp->dbtgc', lax.complex(c_re, c_im), states).real


def s5_mixer(u_ctx, u_lat, ctx_out, lam_re, lam_im, log_dt, b_re, b_im, c_re, c_im, d_skip):
    b_, tc, d = u_ctx.shape
    seqs = jnp.stack([cat_streams(u_ctx, u_lat, False), cat_streams(u_ctx, u_lat, True)]).astype(jnp.float32)
    t = seqs.shape[2]
    nb = S5_GROUPS // S5_BLOCK
    u_blk = seqs.reshape(2, b_, t, nb, S5_BLOCK, S5_GROUP).transpose(3, 0, 1, 2, 4, 5)

    def blk(p):
        p = p.astype(jnp.float32)
        return jnp.swapaxes(p.reshape((2, nb, S5_BLOCK) + p.shape[2:]), 0, 1)

    ys = lax.map(s5_block, (u_blk, blk(lam_re), blk(lam_im), blk(log_dt), blk(b_re), blk(b_im), blk(c_re), blk(c_im)))
    y = ys.transpose(1, 2, 3, 0, 4, 5).reshape(2, b_, t, d)
    d_skip = d_skip.astype(jnp.float32)

    def out(y_f, y_b_rev, u):
        return jax.nn.gelu(y_f + y_b_rev[:, ::-1] + d_skip * u).astype(u.dtype)

    f_lat = out(y[0, :, tc:], y[1, :, tc:], u_lat)
    f_ctx = out(y[0, :, :tc], y[1, :, :tc], u_ctx) if ctx_out else None
    return f_ctx, f_lat


def glu_out(f, w):
    gv = f @ w
    return gv[..., :D_MODEL] * jax.nn.sigmoid(gv[..., D_MODEL:])


def expert_choice_ffn(h, w_router, w_in, w_out):
    b_, n, d = h.shape
    cap = EC_CAPACITY * n // N_EXPERTS
    aff = jax.nn.softmax(jnp.einsum('bnd,de->bne', h, w_router).astype(jnp.float32), axis=-1)
    gate, idx = lax.top_k(jnp.swapaxes(aff, 1, 2), cap)
    xs = jax.vmap(lambda hb, ib: hb[ib])(h, idx)
    gu = jnp.einsum('becd,edf->becf', xs, w_in)
    act = jax.nn.silu(gu[..., :EXPERT_FF]) * gu[..., EXPERT_FF:]
    out = jnp.einsum('becf,efd->becd', act, w_out) * gate[..., None].astype(h.dtype)
    return jax.vmap(lambda ib, ob: jnp.zeros((n, d), ob.dtype).at[ib.reshape(-1)].add(ob.reshape(-1, d)))(idx, out)


def setup_inputs(seed: int = 0) -> dict:
    key = jax.random.key(seed)
    ks = iter(jax.random.split(key, 48))
    f32 = jnp.float32

    def nrm(shape, std):
        return std * jax.random.normal(next(ks), shape, f32)

    def unif(shape, lo, hi):
        return jax.random.uniform(next(ks), shape, f32, lo, hi)

    d = D_MODEL
    x = nrm((BATCH, SEQ, d), 1.0)
    c = nrm((BATCH, d), 1.0)
    ctx = nrm((BATCH, CTX_LEN, d), 1.0)
    c_ctx = nrm((d,), 1.0)
    ada_w = nrm((DEPTH, d, 6 * d), 0.5 * d ** -0.5)
    ada_b = nrm((DEPTH, 6 * d), 0.01)
    ln_g = 1.0 + nrm((DEPTH, 2, d), 0.01)
    ln_b = nrm((DEPTH, 2, d), 0.01)
    ev_w_in = nrm((N_EVEN, d, EVEN_IN), d ** -0.5)
    ev_w_out = nrm((N_EVEN, EVEN_MIX, d), EVEN_MIX ** -0.5 * DEEPNORM_BETA)
    hy_conv = nrm((N_EVEN, HY_SHORT, 3 * HY_WIDTH), HY_SHORT ** -0.5)
    hy_w1 = nrm((N_EVEN, HY_EMB, HY_FILTER_FF), HY_EMB ** -0.5)
    hy_b1 = nrm((N_EVEN, HY_FILTER_FF), 0.1)
    hy_w2 = nrm((N_EVEN, HY_FILTER_FF, HY_FILTER_FF), HY_FILTER_FF ** -0.5)
    hy_b2 = nrm((N_EVEN, HY_FILTER_FF), 0.1)
    hy_w3 = nrm((N_EVEN, HY_FILTER_FF, HY_FILTER_FF), HY_FILTER_FF ** -0.5)
    hy_b3 = nrm((N_EVEN, HY_FILTER_FF), 0.1)
    hy_w4 = nrm((N_EVEN, HY_FILTER_FF, HY_ORDER * 2 * HY_WIDTH), 0.1 * HY_FILTER_FF ** -0.5)
    hy_freq = 1.0 + nrm((N_EVEN, 3, HY_FILTER_FF), 0.01)
    hy_bias = nrm((N_EVEN, HY_ORDER, HY_WIDTH), 1.0)
    dn_conv = nrm((N_EVEN, DN_SHORT, 3 * DN_WIDTH), DN_SHORT ** -0.5)
    dn_a_log = jnp.log(unif((N_EVEN, 2, DN_HEADS), 1.0, 16.0))
    dt = jnp.exp(unif((N_EVEN, 2, DN_HEADS), math.log(1e-3), math.log(1e-1)))
    dn_dt_bias = dt + jnp.log(-jnp.expm1(-dt))
    dn_norm_w = 1.0 + nrm((N_EVEN, DN_HEAD_DIM), 0.01)
    s5_lam_re = -0.5 + nrm((N_ODD, 2, S5_GROUPS, S5_STATE), 0.01)
    s5_lam_im = math.pi * jnp.arange(S5_STATE, dtype=f32) + nrm((N_ODD, 2, S5_GROUPS, S5_STATE), 0.01)
    s5_log_dt = unif((N_ODD, 2, S5_GROUPS), math.log(1e-3), math.log(1e-1))
    s5_b_re = nrm((N_ODD, 2, S5_GROUPS, S5_STATE, S5_GROUP), (2 * S5_GROUP) ** -0.5)
    s5_b_im = nrm((N_ODD, 2, S5_GROUPS, S5_STATE, S5_GROUP), (2 * S5_GROUP) ** -0.5)
    s5_c_re = nrm((N_ODD, 2, S5_GROUPS, S5_GROUP, S5_STATE), S5_STATE ** -0.5)
    s5_c_im = nrm((N_ODD, 2, S5_GROUPS, S5_GROUP, S5_STATE), S5_STATE ** -0.5)
    s5_d = nrm((N_ODD, d), 0.5)
    od_w_glu = nrm((N_ODD, d, 2 * d), d ** -0.5) * jnp.where(jnp.arange(2 * d) < d, DEEPNORM_BETA, 1.0)
    moe_router = nrm((DEPTH, d, N_EXPERTS), d ** -0.5)
    moe_w_in = nrm((DEPTH, N_EXPERTS, d, 2 * EXPERT_FF), d ** -0.5)
    moe_w_out = nrm((DEPTH, N_EXPERTS, EXPERT_FF, d), EXPERT_FF ** -0.5 * DEEPNORM_BETA)
    return {'x': x, 'c': c, 'ctx': ctx, 'c_ctx': c_ctx, 'ada_w': ada_w, 'ada_b': ada_b, 'ln_g': ln_g, 'ln_b': ln_b, 'ev_w_in': ev_w_in, 'ev_w_out': ev_w_out, 'hy_conv': hy_conv, 'hy_w1': hy_w1, 'hy_b1': hy_b1, 'hy_w2': hy_w2, 'hy_b2': hy_b2, 'hy_w3': hy_w3, 'hy_b3': hy_b3, 'hy_w4': hy_w4, 'hy_freq': hy_freq, 'hy_bias': hy_bias, 'dn_conv': dn_conv, 'dn_a_log': dn_a_log, 'dn_dt_bias': dn_dt_bias, 'dn_norm_w': dn_norm_w, 's5_lam_re': s5_lam_re, 's5_lam_im': s5_lam_im, 's5_log_dt': s5_log_dt, 's5_b_re': s5_b_re, 's5_b_im': s5_b_im, 's5_c_re': s5_c_re, 's5_c_im': s5_c_im, 's5_d': s5_d, 'od_w_glu': od_w_glu, 'moe_router': moe_router, 'moe_w_in': moe_w_in, 'moe_w_out': moe_w_out}


def reference(x, c, ctx, c_ctx, ada_w, ada_b, ln_g, ln_b, ev_w_in, ev_w_out, hy_conv, hy_w1, hy_b1, hy_w2, hy_b2, hy_w3, hy_b3, hy_w4, hy_freq, hy_bias, dn_conv, dn_a_log, dn_dt_bias, dn_norm_w, s5_lam_re, s5_lam_im, s5_log_dt, s5_b_re, s5_b_im, s5_c_re, s5_c_im, s5_d, od_w_glu, moe_router, moe_w_in, moe_w_out):
    d = D_MODEL
    h_lat, h_ctx = x, ctx
    s_lat = jax.nn.silu(c)
    s_ctx = jax.nn.silu(c_ctx)[None]
    for l in range(DEPTH):
        last = l == DEPTH - 1
        col = (l // 2) % 2 == 1
        i = l // 2
        n_ctx_mod = 2 if last else 6
        mod_lat = jnp.split((s_lat @ ada_w[l] + ada_b[l])[:, None], 6, axis=-1)
        mod_ctx = jnp.split((s_ctx @ ada_w[l][:, :n_ctx_mod * d] + ada_b[l][:n_ctx_mod * d])[:, None], n_ctx_mod, axis=-1)
        u_lat = h_lat * (1.0 + mod_lat[1]) + mod_lat[0]
        u_ctx = h_ctx * (1.0 + mod_ctx[1]) + mod_ctx[0]
        if col:
            u_lat = to_col_major(u_lat)
        if l % 2 == 0:
            f_ctx, f_lat = even_mixer(u_ctx, u_lat, not last, ev_w_in[i], hy_conv[i], hy_w1[i], hy_b1[i], hy_w2[i], hy_b2[i], hy_w3[i], hy_b3[i], hy_w4[i], hy_freq[i], hy_bias[i], dn_conv[i], dn_a_log[i], dn_dt_bias[i], dn_norm_w[i])
            w_proj = ev_w_out[i]
            proj = lambda f, w: f @ w
        else:
            f_ctx, f_lat = s5_mixer(u_ctx, u_lat, not last, s5_lam_re[i], s5_lam_im[i], s5_log_dt[i], s5_b_re[i], s5_b_im[i], s5_c_re[i], s5_c_im[i], s5_d[i])
            w_proj = od_w_glu[i]
            proj = glu_out
        if col:
            f_lat = from_col_major(f_lat)
        h_lat = post_norm(h_lat, mod_lat[2] * proj(f_lat, w_proj), ln_g[l, 0], ln_b[l, 0])
        m_lat = expert_choice_ffn(h_lat * (1.0 + mod_lat[4]) + mod_lat[3], moe_router[l], moe_w_in[l], moe_w_out[l])
        h_lat = post_norm(h_lat, mod_lat[5] * m_lat, ln_g[l, 1], ln_b[l, 1])
        if not last:
            h_ctx = post_norm(h_ctx, mod_ctx[2] * proj(f_ctx, w_proj), ln_g[l, 0], ln_b[l, 0])
            m_ctx = expert_choice_ffn(h_ctx * (1.0 + mod_ctx[4]) + mod_ctx[3], moe_router[l], moe_w_in[l], moe_w_out[l])
            h_ctx = post_norm(h_ctx, mod_ctx[5] * m_ctx, ln_g[l, 1], ln_b[l, 1])
    return h_lat
```

```python
import functools
import math

import jax
import jax.numpy as jnp
from jax import lax
from jax.experimental import pallas as pl
from jax.experimental.pallas import tpu as pltpu

D_MODEL = 4096
DEPTH = 4
CTX_LEN = 256
GRID_W = 64
HY_WIDTH = D_MODEL // 2
HY_ORDER = 2
HY_EMB = 33
HY_BANDS = (HY_EMB - 1) // 2
HY_TARGET = 1e-2
HY_MAX_DECAY = math.log(HY_TARGET) / 0.3
HY_MIN_DECAY = math.log(HY_TARGET) / 1.5
DN_HEADS = 16
DN_HEAD_DIM = 128
DN_WIDTH = DN_HEADS * DN_HEAD_DIM
DN_CHUNK = 64
S5_GROUP = 16
S5_GROUPS = D_MODEL // S5_GROUP
S5_STATE = 64
S5_BLOCK = 32
S5_MAX_RE = -1e-4
N_EXPERTS = 16
EXPERT_FF = 384
EC_CAPACITY = 2
DEEPNORM_ALPHA = (2 * DEPTH) ** 0.25
LN_EPS = 1e-5

VMEM_LIMIT_BYTES = 56 * 1024 * 1024


def _mm_body(a_ref, w_ref, o_ref, wb_ref):
    @pl.when(pl.program_id(2) == 0)
    def _():
        wb_ref[...] = w_ref[...].astype(jnp.bfloat16)

    o_ref[...] = jnp.dot(a_ref[...].astype(jnp.bfloat16), wb_ref[...], preferred_element_type=jnp.float32).astype(o_ref.dtype)


def _pick(n, cands):
    for c in cands:
        if n % c == 0:
            return c
    return n


def _bmm(a, w, out_dtype=jnp.float32):
    e, m, k = a.shape
    n = w.shape[-1]
    tm = _pick(m, (512, 256, 128, 64, 32, 16, 8))
    tn = _pick(n, (512, 384, 256, 128))
    return pl.pallas_call(
        _mm_body,
        grid=(e, n // tn, m // tm),
        in_specs=[
            pl.BlockSpec((None, tm, k), lambda ei, j, i: (ei, i, 0)),
            pl.BlockSpec((None, k, tn), lambda ei, j, i: (ei, 0, j)),
        ],
        out_specs=pl.BlockSpec((None, tm, tn), lambda ei, j, i: (ei, i, j)),
        out_shape=jax.ShapeDtypeStruct((e, m, n), out_dtype),
        scratch_shapes=[pltpu.VMEM((k, tn), jnp.bfloat16)],
        compiler_params=pltpu.CompilerParams(
            dimension_semantics=("arbitrary", "arbitrary", "arbitrary"),
            vmem_limit_bytes=VMEM_LIMIT_BYTES),
        name="bmm",
    )(a, w)


def _mm(a, w):
    lead = a.shape[:-1]
    out = _bmm(a.reshape(1, -1, a.shape[-1]), w[None])
    return out.reshape(lead + (w.shape[-1],))


def post_norm(h, y, g, b):
    z = (DEEPNORM_ALPHA * h + y).astype(jnp.float32)
    mu = jnp.mean(z, axis=-1, keepdims=True)
    var = jnp.mean(jnp.square(z - mu), axis=-1, keepdims=True)
    return ((z - mu) * lax.rsqrt(var + LN_EPS) * g.astype(jnp.float32) + b.astype(jnp.float32)).astype(h.dtype)


def short_conv(u, w):
    k, ch = w.shape
    return lax.conv_general_dilated(u, w.astype(u.dtype)[:, None, :], window_strides=(1,), padding=[(k // 2, k // 2)], dimension_numbers=('NWC', 'WIO', 'NWC'), feature_group_count=ch)


def cat_streams(a_ctx, a_lat, rev):
    if rev:
        return jnp.concatenate([a_ctx[:, ::-1], a_lat[:, ::-1]], axis=1)
    return jnp.concatenate([a_ctx, a_lat], axis=1)


def to_col_major(h):
    b_, n, d = h.shape
    rows = n // GRID_W
    return h.reshape(b_, rows, GRID_W, d).transpose(0, 2, 1, 3).reshape(b_, n, d)


def from_col_major(h):
    b_, n, d = h.shape
    rows = n // GRID_W
    return h.reshape(b_, GRID_W, rows, d).transpose(0, 2, 1, 3).reshape(b_, n, d)


def hyena_filters(L, w1, b1, w2, b2, w3, b3, w4, freq):
    f32 = jnp.float32
    pos = jnp.arange(L, dtype=f32)
    t = pos / max(L - 1, 1)
    ang = (2.0 * math.pi / L) * pos
    bands = jnp.linspace(1e-4, HY_BANDS - 1, HY_BANDS, dtype=f32)
    z = jnp.concatenate([t[:, None], jnp.cos(ang[:, None] * bands), -jnp.sin(ang[:, None] * bands)], axis=-1)
    freq = freq.astype(f32)
    h = jnp.sin(freq[0] * (z @ w1.astype(f32) + b1.astype(f32)))
    h = jnp.sin(freq[1] * (h @ w2.astype(f32) + b2.astype(f32)))
    h = jnp.sin(freq[2] * (h @ w3.astype(f32) + b3.astype(f32)))
    h = (h @ w4.astype(f32)).reshape(L, HY_ORDER, 2, HY_WIDTH)
    deltas = jnp.abs(jnp.linspace(HY_MIN_DECAY, HY_MAX_DECAY, HY_WIDTH, dtype=f32))
    h = h * jnp.exp(-t[:, None, None, None] * deltas)
    fwd, bwd = h[:, :, 0], h[:, :, 1]
    return jnp.concatenate([fwd, jnp.zeros_like(fwd[:1]), bwd[:0:-1]], axis=0)


def long_conv(u, filt, bias):
    L = u.shape[1]
    uf = jnp.fft.rfft(u.astype(jnp.float32), n=2 * L, axis=1)
    hf = jnp.fft.rfft(filt, axis=0)
    y = jnp.fft.irfft(uf * hf, n=2 * L, axis=1)[:, :L]
    return (y + u * bias).astype(u.dtype)


def hyena(hy, conv_w, filt_params, bias):
    L = hy.shape[1]
    v, x1, x2 = jnp.split(short_conv(hy, conv_w), 3, axis=-1)
    filt = hyena_filters(L, *filt_params)
    bias = bias.astype(jnp.float32)
    z = x1 * long_conv(v, filt[:, 0], bias[0])
    return x2 * long_conv(z, filt[:, 1], bias[1])


def l2norm(a):
    return a * lax.rsqrt(jnp.sum(a * a, axis=-1, keepdims=True) + 1e-6)


def gated_delta_chunked(q, k, v, g, beta):
    b_, t, h, dk = q.shape
    dv = v.shape[-1]
    n = t // DN_CHUNK

    def blocks(a):
        a = a.astype(jnp.float32).reshape((b_, n, DN_CHUNK, h) + a.shape[3:])
        return jnp.moveaxis(a, (1, 3), (0, 2))

    qc, kc, vc, bc = blocks(q), blocks(k), blocks(v), blocks(beta)
    gc = jnp.cumsum(blocks(g), axis=-1)
    causal = jnp.tril(jnp.ones((DN_CHUNK, DN_CHUNK), bool))
    strict = jnp.tril(jnp.ones((DN_CHUNK, DN_CHUNK), bool), -1)
    decay = jnp.exp(jnp.where(causal, gc[..., :, None] - gc[..., None, :], -jnp.inf))
    kb = kc * bc[..., None]
    a_mat = jnp.where(strict, jnp.einsum('nbhid,nbhjd->nbhij', kb, kc) * decay, 0.0)
    t_mat = a_mat + jnp.eye(DN_CHUNK, dtype=jnp.float32)
    u = lax.linalg.triangular_solve(t_mat, vc * bc[..., None], left_side=True, lower=True, unit_diagonal=True)
    w = lax.linalg.triangular_solve(t_mat, kb * jnp.exp(gc)[..., None], left_side=True, lower=True, unit_diagonal=True)
    qk = jnp.einsum('nbhid,nbhjd->nbhij', qc, kc) * decay

    def step(state, xs):
        q_i, k_i, u_i, w_i, g_i, qk_i = xs
        v_new = u_i - jnp.einsum('bhcd,bhde->bhce', w_i, state)
        o = jnp.einsum('bhcd,bhde->bhce', q_i * jnp.exp(g_i)[..., None], state) + jnp.einsum('bhij,bhje->bhie', qk_i, v_new)
        g_last = g_i[..., -1:]
        k_dec = k_i * jnp.exp(g_last - g_i)[..., None]
        state = state * jnp.exp(g_last)[..., None] + jnp.einsum('bhcd,bhce->bhde', k_dec, v_new)
        return state, o

    s0 = jnp.zeros((b_, h, dk, dv), jnp.float32)
    _, o = lax.scan(step, s0, (qc, kc, u, w, gc, qk))
    return jnp.moveaxis(o, (0, 2), (1, 3)).reshape(b_, t, h, dv)


_HI = lax.Precision.HIGHEST


def _dot(a, b):
    return jnp.dot(a, b, precision=_HI, preferred_element_type=jnp.float32)


def _dot_nt(a, b):
    return lax.dot_general(a, b, (((1,), (1,)), ((), ())), precision=_HI, preferred_element_type=jnp.float32)


def _dot_tn(a, b):
    return lax.dot_general(a, b, (((0,), (0,)), ((), ())), precision=_HI, preferred_element_type=jnp.float32)


def _delta_body(q_ref, k_ref, v_ref, gcol_ref, grow_ref, bcol_ref, o_ref, s_ref):
    rev = pl.program_id(0) >= DN_HEADS

    @pl.when(pl.program_id(1) == 0)
    def _():
        s_ref[...] = jnp.zeros_like(s_ref)

    c = DN_CHUNK
    row = lax.broadcasted_iota(jnp.int32, (c, c), 0)
    col = lax.broadcasted_iota(jnp.int32, (c, c), 1)
    lag = jnp.where(rev, col - row, row - col)
    incl = lag >= 0
    strict = lag > 0
    incl_f = incl.astype(jnp.float32)
    eye = (row == col).astype(jnp.float32)

    q = q_ref[...]
    k = k_ref[...]
    v = v_ref[...]
    g_col = gcol_ref[...]
    g_row = grow_ref[...]
    beta = bcol_ref[...]

    gc_col = _dot(incl_f, g_col)
    gc_row = _dot_nt(g_row, incl_f)
    g_tot = jnp.sum(g_col, axis=0, keepdims=True)
    decay = jnp.exp(jnp.where(incl, gc_col - gc_row, -1e30))

    kb = k * beta
    a_mat = jnp.where(strict, _dot_nt(kb, k) * decay, 0.0)
    x = eye - a_mat
    p = _dot(a_mat, a_mat)
    n_sq = c.bit_length() - 2
    for it in range(n_sq):
        x = x + _dot(x, p)
        if it + 1 < n_sq:
            p = _dot(p, p)
    u = _dot(x, v * beta)
    w = _dot(x, kb * jnp.exp(gc_col))
    qk = _dot_nt(q, k) * decay

    s = s_ref[...]
    v_new = u - _dot(w, s)
    o_ref[...] = _dot(q * jnp.exp(gc_col), s) + _dot(qk, v_new)
    k_dec = k * jnp.exp(g_tot - gc_col)
    s_ref[...] = s * jnp.exp(g_tot) + _dot_tn(k_dec, v_new)


def _rev_chunk(cc, n_ctx_chunks, n_chunks):
    return jnp.where(cc < n_ctx_chunks, n_ctx_chunks - 1 - cc, n_chunks + n_ctx_chunks - 1 - cc)


def gated_delta_pallas(q, k, v, g, beta, n_ctx):
    t = q.shape[0]
    nch = t // DN_CHUNK
    ncc = n_ctx // DN_CHUNK
    dh = 2 * DN_HEADS
    g_col = jnp.transpose(g, (0, 2, 1)).reshape(dh, t, 1)
    g_row = g_col.reshape(dh, nch, 1, DN_CHUNK)
    b_col = jnp.transpose(beta, (0, 2, 1)).reshape(dh, t, 1)

    def chunk(i, cc):
        return jnp.where(i >= DN_HEADS, _rev_chunk(cc, ncc, nch), cc)

    qkv_spec = pl.BlockSpec((DN_CHUNK, DN_HEAD_DIM), lambda i, cc: (chunk(i, cc), i % DN_HEADS))
    col_spec = pl.BlockSpec((None, DN_CHUNK, 1), lambda i, cc: (i, chunk(i, cc), 0))
    return pl.pallas_call(
        _delta_body,
        grid=(dh, nch),
        in_specs=[qkv_spec, qkv_spec, qkv_spec, col_spec,
                  pl.BlockSpec((None, None, 1, DN_CHUNK), lambda i, cc: (i, chunk(i, cc), 0, 0)),
                  col_spec],
        out_specs=pl.BlockSpec((None, DN_CHUNK, DN_HEAD_DIM), lambda i, cc: (i // DN_HEADS, chunk(i, cc), i % DN_HEADS)),
        out_shape=jax.ShapeDtypeStruct((2, t, DN_WIDTH), jnp.float32),
        scratch_shapes=[pltpu.VMEM((DN_HEAD_DIM, DN_HEAD_DIM), jnp.float32)],
        compiler_params=pltpu.CompilerParams(dimension_semantics=("arbitrary", "arbitrary")),
        name="gated_delta",
    )(q, k, v, g_col, g_row, b_col)


def gated_head_norm(o, gate, w):
    b_, t_ = gate.shape[:2]
    o = o * lax.rsqrt(jnp.mean(o * o, axis=-1, keepdims=True) + 1e-6) * w.astype(jnp.float32)
    o = o * jax.nn.silu(gate.astype(jnp.float32)).reshape(b_, t_, DN_HEADS, DN_HEAD_DIM)
    return o.reshape(b_, t_, DN_WIDTH).astype(gate.dtype)


def deltanet(dn_ctx, dn_lat, ctx_out, conv_w, a_log, dt_bias, norm_w):
    a_rate = jnp.exp(a_log.astype(jnp.float32))
    dt_bias = dt_bias.astype(jnp.float32)

    def prep(dn):
        b_, t_, _ = dn.shape
        qkv = jax.nn.silu(short_conv(dn[..., :3 * DN_WIDTH], conv_w)).astype(jnp.float32)
        q, k, v = (a.reshape(b_, t_, DN_HEADS, DN_HEAD_DIM) for a in jnp.split(qkv, 3, axis=-1))
        ab = dn[..., 4 * DN_WIDTH:].astype(jnp.float32).reshape(b_, t_, 2, 2, DN_HEADS)
        beta = jax.nn.sigmoid(ab[:, :, 0])
        g = -a_rate * jax.nn.softplus(ab[:, :, 1] + dt_bias)
        return l2norm(q) * DN_HEAD_DIM ** -0.5, l2norm(k), v, beta, g

    pc, pl_ = prep(dn_ctx), prep(dn_lat)
    tc = dn_ctx.shape[1]
    b_ = dn_lat.shape[0]
    assert b_ == 1
    uni = [jnp.concatenate([a, b], axis=1)[0] for a, b in zip(pc, pl_)]
    t_all = uni[0].shape[0]
    o2 = gated_delta_pallas(uni[0].reshape(t_all, DN_WIDTH), uni[1].reshape(t_all, DN_WIDTH), uni[2].reshape(t_all, DN_WIDTH),
                            jnp.transpose(uni[4], (1, 0, 2)), jnp.transpose(uni[3], (1, 0, 2)), tc)
    o = (o2[0] + o2[1]).reshape(1, t_all, DN_HEADS, DN_HEAD_DIM)
    o_ctx, o_lat = o[:, :tc], o[:, tc:]
    y_lat = gated_head_norm(o_lat, dn_lat[..., 3 * DN_WIDTH:4 * DN_WIDTH], norm_w)
    y_ctx = gated_head_norm(o_ctx, dn_ctx[..., 3 * DN_WIDTH:4 * DN_WIDTH], norm_w) if ctx_out else None
    return y_ctx, y_lat


def even_mixer(u_ctx, u_lat, ctx_out, w_in, hy_conv, hy_w1, hy_b1, hy_w2, hy_b2, hy_w3, hy_b3, hy_w4, hy_freq, hy_bias, dn_conv, dn_a_log, dn_dt_bias, dn_norm_w):
    hy_cols = 3 * HY_WIDTH
    n_main = 7 * HY_WIDTH
    filt = (hy_w1, hy_b1, hy_w2, hy_b2, hy_w3, hy_b3, hy_w4, hy_freq)
    w_tail = w_in[:, n_main:]

    def proj_in(u):
        lead = u.shape[:-1]
        a = u.reshape(1, -1, u.shape[-1])
        m = a.shape[1]
        tm = _pick(m, (512, 256))
        tn = 512
        main = pl.pallas_call(
            _mm_body,
            grid=(1, n_main // tn, m // tm),
            in_specs=[
                pl.BlockSpec((None, tm, D_MODEL), lambda ei, j, i: (ei, i, 0)),
                pl.BlockSpec((None, D_MODEL, tn), lambda ei, j, i: (ei, 0, j)),
            ],
            out_specs=pl.BlockSpec((None, tm, tn), lambda ei, j, i: (ei, i, j)),
            out_shape=jax.ShapeDtypeStruct((1, m, n_main), jnp.float32),
            scratch_shapes=[pltpu.VMEM((D_MODEL, tn), jnp.bfloat16)],
            compiler_params=pltpu.CompilerParams(
                dimension_semantics=("arbitrary", "arbitrary", "arbitrary"),
                vmem_limit_bytes=VMEM_LIMIT_BYTES),
            name="proj_in",
        )(a, w_in[None])
        tail = _bmm(a, w_tail[None])
        return jnp.concatenate([main, tail], axis=-1).reshape(lead + (w_in.shape[-1],))

    z_lat = proj_in(u_lat)
    z_ctx = proj_in(u_ctx)
    dn_ctx = z_ctx[..., hy_cols:]
    y_dn_ctx, y_dn_lat = deltanet(dn_ctx, z_lat[..., hy_cols:], ctx_out, dn_conv, dn_a_log, dn_dt_bias, dn_norm_w)
    y_lat = jnp.concatenate([hyena(z_lat[..., :hy_cols], hy_conv, filt, hy_bias), y_dn_lat], axis=-1)
    y_ctx = jnp.concatenate([hyena(z_ctx[..., :hy_cols], hy_conv, filt, hy_bias), y_dn_ctx], axis=-1) if ctx_out else None
    return y_ctx, y_lat


def s5_combine(left, right):
    a_l, b_l = left
    a_r, b_r = right
    return a_l * a_r, a_r * b_l + b_r


def s5_block(args):
    u, lam_re, lam_im, log_dt, b_re, b_im, c_re, c_im = args
    lam = lax.complex(jnp.minimum(lam_re, S5_MAX_RE), lam_im)
    lam_bar = jnp.exp(lam * jnp.exp(log_dt)[..., None])
    b_bar = ((lam_bar - 1.0) / lam)[..., None] * lax.complex(b_re, b_im)
    bu = jnp.einsum('dgpc,dbtgc->dbtgp', b_bar, u.astype(jnp.complex64))
    a = jnp.broadcast_to(lam_bar[:, None, None], bu.shape)
    _, states = lax.associative_scan(s5_combine, (a, bu), axis=2)
    return jnp.einsum('dgcp,dbtgp->dbtgc', lax.complex(c_re, c_im), states).real


def s5_mixer(u_ctx, u_lat, ctx_out, lam_re, lam_im, log_dt, b_re, b_im, c_re, c_im, d_skip):
    b_, tc, d = u_ctx.shape
    seqs = jnp.stack([cat_streams(u_ctx, u_lat, False), cat_streams(u_ctx, u_lat, True)]).astype(jnp.float32)
    t = seqs.shape[2]
    nb = S5_GROUPS // S5_BLOCK
    u_blk = seqs.reshape(2, b_, t, nb, S5_BLOCK, S5_GROUP).transpose(3, 0, 1, 2, 4, 5)

    def blk(p):
        p = p.astype(jnp.float32)
        return jnp.swapaxes(p.reshape((2, nb, S5_BLOCK) + p.shape[2:]), 0, 1)

    ys = lax.map(s5_block, (u_blk, blk(lam_re), blk(lam_im), blk(log_dt), blk(b_re), blk(b_im), blk(c_re), blk(c_im)))
    y = ys.transpose(1, 2, 3, 0, 4, 5).reshape(2, b_, t, d)
    d_skip = d_skip.astype(jnp.float32)

    def out(y_f, y_b_rev, u):
        return jax.nn.gelu(y_f + y_b_rev[:, ::-1] + d_skip * u).astype(u.dtype)

    f_lat = out(y[0, :, tc:], y[1, :, tc:], u_lat)
    f_ctx = out(y[0, :, :tc], y[1, :, :tc], u_ctx) if ctx_out else None
    return f_ctx, f_lat


def glu_out(f, w):
    gv = _mm(f, w)
    return gv[..., :D_MODEL] * jax.nn.sigmoid(gv[..., D_MODEL:])


def expert_choice_ffn(h, w_router, w_in, w_out):
    b_, n, d = h.shape
    cap = EC_CAPACITY * n // N_EXPERTS
    aff = jax.nn.softmax(jnp.einsum('bnd,de->bne', h, w_router).astype(jnp.float32), axis=-1)
    gate, idx = lax.top_k(jnp.swapaxes(aff, 1, 2), cap)
    xs = jax.vmap(lambda hb, ib: hb[ib])(h, idx)
    gu = _bmm(xs[0], w_in)
    act = jax.nn.silu(gu[..., :EXPERT_FF]) * gu[..., EXPERT_FF:]
    out = (_bmm(act, w_out) * gate[0][..., None].astype(h.dtype))[None]
    return jax.vmap(lambda ib, ob: jnp.zeros((n, d), ob.dtype).at[ib.reshape(-1)].add(ob.reshape(-1, d)))(idx, out)


def kernel(x, c, ctx, c_ctx, ada_w, ada_b, ln_g, ln_b, ev_w_in, ev_w_out, hy_conv, hy_w1, hy_b1, hy_w2, hy_b2, hy_w3, hy_b3, hy_w4, hy_freq, hy_bias, dn_conv, dn_a_log, dn_dt_bias, dn_norm_w, s5_lam_re, s5_lam_im, s5_log_dt, s5_b_re, s5_b_im, s5_c_re, s5_c_im, s5_d, od_w_glu, moe_router, moe_w_in, moe_w_out):
    d = D_MODEL
    h_lat, h_ctx = x, ctx
    s_lat = jax.nn.silu(c)
    s_ctx = jax.nn.silu(c_ctx)[None]
    for l in range(DEPTH):
        last = l == DEPTH - 1
        col = (l // 2) % 2 == 1
        i = l // 2
        n_ctx_mod = 2 if last else 6
        mod_lat = jnp.split((s_lat @ ada_w[l] + ada_b[l])[:, None], 6, axis=-1)
        mod_ctx = jnp.split((s_ctx @ ada_w[l][:, :n_ctx_mod * d] + ada_b[l][:n_ctx_mod * d])[:, None], n_ctx_mod, axis=-1)
        u_lat = h_lat * (1.0 + mod_lat[1]) + mod_lat[0]
        u_ctx = h_ctx * (1.0 + mod_ctx[1]) + mod_ctx[0]
        if col:
            u_lat = to_col_major(u_lat)
        if l % 2 == 0:
            f_ctx, f_lat = even_mixer(u_ctx, u_lat, not last, ev_w_in[i], hy_conv[i], hy_w1[i], hy_b1[i], hy_w2[i], hy_b2[i], hy_w3[i], hy_b3[i], hy_w4[i], hy_freq[i], hy_bias[i], dn_conv[i], dn_a_log[i], dn_dt_bias[i], dn_norm_w[i])
            w_proj = ev_w_out[i]
            proj = _mm
        else:
            f_ctx, f_lat = s5_mixer(u_ctx, u_lat, not last, s5_lam_re[i], s5_lam_im[i], s5_log_dt[i], s5_b_re[i], s5_b_im[i], s5_c_re[i], s5_c_im[i], s5_d[i])
            w_proj = od_w_glu[i]
            proj = glu_out
        if col:
            f_lat = from_col_major(f_lat)
        h_lat = post_norm(h_lat, mod_lat[2] * proj(f_lat, w_proj), ln_g[l, 0], ln_b[l, 0])
        m_lat = expert_choice_ffn(h_lat * (1.0 + mod_lat[4]) + mod_lat[3], moe_router[l], moe_w_in[l], moe_w_out[l])
        h_lat = post_norm(h_lat, mod_lat[5] * m_lat, ln_g[l, 1], ln_b[l, 1])
        if not last:
            h_ctx = post_norm(h_ctx, mod_ctx[2] * proj(f_ctx, w_proj), ln_g[l, 0], ln_b[l, 0])
            m_ctx = expert_choice_ffn(h_ctx * (1.0 + mod_ctx[4]) + mod_ctx[3], moe_router[l], moe_w_in[l], moe_w_out[l])
            h_ctx = post_norm(h_ctx, mod_ctx[5] * m_ctx, ln_g[l, 1], ln_b[l, 1])
    return h_lat
```

```python
import functools
import math

import jax
import jax.numpy as jnp
from jax import lax
from jax.experimental import pallas as pl
from jax.experimental.pallas import tpu as pltpu

D_MODEL = 4096
DEPTH = 4
CTX_LEN = 256
GRID_W = 64
HY_WIDTH = D_MODEL // 2
HY_ORDER = 2
HY_EMB = 33
HY_BANDS = (HY_EMB - 1) // 2
HY_TARGET = 1e-2
HY_MAX_DECAY = math.log(HY_TARGET) / 0.3
HY_MIN_DECAY = math.log(HY_TARGET) / 1.5
DN_HEADS = 16
DN_HEAD_DIM = 128
DN_WIDTH = DN_HEADS * DN_HEAD_DIM
DN_CHUNK = 64
S5_GROUP = 16
S5_GROUPS = D_MODEL // S5_GROUP
S5_STATE = 64
S5_BLOCK = 32
S5_MAX_RE = -1e-4
N_EXPERTS = 16
EXPERT_FF = 384
EC_CAPACITY = 2
DEEPNORM_ALPHA = (2 * DEPTH) ** 0.25
LN_EPS = 1e-5

VMEM_LIMIT_BYTES = 56 * 1024 * 1024


def _mm_body(a_ref, w_ref, o_ref, wb_ref):
    @pl.when(pl.program_id(2) == 0)
    def _():
        wb_ref[...] = w_ref[...].astype(jnp.bfloat16)

    o_ref[...] = jnp.dot(a_ref[...].astype(jnp.bfloat16), wb_ref[...], preferred_element_type=jnp.float32).astype(o_ref.dtype)


def _pick(n, cands):
    for c in cands:
        if n % c == 0:
            return c
    return n


def _bmm(a, w, out_dtype=jnp.float32):
    e, m, k = a.shape
    n = w.shape[-1]
    tm = _pick(m, (512, 256, 128, 64, 32, 16, 8))
    tn = _pick(n, (512, 384, 256, 128))
    return pl.pallas_call(
        _mm_body,
        grid=(e, n // tn, m // tm),
        in_specs=[
            pl.BlockSpec((None, tm, k), lambda ei, j, i: (ei, i, 0)),
            pl.BlockSpec((None, k, tn), lambda ei, j, i: (ei, 0, j)),
        ],
        out_specs=pl.BlockSpec((None, tm, tn), lambda ei, j, i: (ei, i, j)),
        out_shape=jax.ShapeDtypeStruct((e, m, n), out_dtype),
        scratch_shapes=[pltpu.VMEM((k, tn), jnp.bfloat16)],
        compiler_params=pltpu.CompilerParams(
            dimension_semantics=("arbitrary", "arbitrary", "arbitrary"),
            vmem_limit_bytes=VMEM_LIMIT_BYTES),
        name="bmm",
    )(a, w)


def _mm(a, w):
    lead = a.shape[:-1]
    out = _bmm(a.reshape(1, -1, a.shape[-1]), w[None])
    return out.reshape(lead + (w.shape[-1],))


def post_norm(h, y, g, b):
    z = (DEEPNORM_ALPHA * h + y).astype(jnp.float32)
    mu = jnp.mean(z, axis=-1, keepdims=True)
    var = jnp.mean(jnp.square(z - mu), axis=-1, keepdims=True)
    return ((z - mu) * lax.rsqrt(var + LN_EPS) * g.astype(jnp.float32) + b.astype(jnp.float32)).astype(h.dtype)


def short_conv(u, w):
    k, ch = w.shape
    return lax.conv_general_dilated(u, w.astype(u.dtype)[:, None, :], window_strides=(1,), padding=[(k // 2, k // 2)], dimension_numbers=('NWC', 'WIO', 'NWC'), feature_group_count=ch)


def cat_streams(a_ctx, a_lat, rev):
    if rev:
        return jnp.concatenate([a_ctx[:, ::-1], a_lat[:, ::-1]], axis=1)
    return jnp.concatenate([a_ctx, a_lat], axis=1)


def to_col_major(h):
    b_, n, d = h.shape
    rows = n // GRID_W
    return h.reshape(b_, rows, GRID_W, d).transpose(0, 2, 1, 3).reshape(b_, n, d)


def from_col_major(h):
    b_, n, d = h.shape
    rows = n // GRID_W
    return h.reshape(b_, GRID_W, rows, d).transpose(0, 2, 1, 3).reshape(b_, n, d)


def hyena_filters(L, w1, b1, w2, b2, w3, b3, w4, freq):
    f32 = jnp.float32
    pos = jnp.arange(L, dtype=f32)
    t = pos / max(L - 1, 1)
    ang = (2.0 * math.pi / L) * pos
    bands = jnp.linspace(1e-4, HY_BANDS - 1, HY_BANDS, dtype=f32)
    z = jnp.concatenate([t[:, None], jnp.cos(ang[:, None] * bands), -jnp.sin(ang[:, None] * bands)], axis=-1)
    freq = freq.astype(f32)
    h = jnp.sin(freq[0] * (z @ w1.astype(f32) + b1.astype(f32)))
    h = jnp.sin(freq[1] * (h @ w2.astype(f32) + b2.astype(f32)))
    h = jnp.sin(freq[2] * (h @ w3.astype(f32) + b3.astype(f32)))
    h = (h @ w4.astype(f32)).reshape(L, HY_ORDER, 2, HY_WIDTH)
    deltas = jnp.abs(jnp.linspace(HY_MIN_DECAY, HY_MAX_DECAY, HY_WIDTH, dtype=f32))
    h = h * jnp.exp(-t[:, None, None, None] * deltas)
    fwd, bwd = h[:, :, 0], h[:, :, 1]
    return jnp.concatenate([fwd, jnp.zeros_like(fwd[:1]), bwd[:0:-1]], axis=0)


def long_conv(u, filt, bias):
    L = u.shape[1]
    uf = jnp.fft.rfft(u.astype(jnp.float32), n=2 * L, axis=1)
    hf = jnp.fft.rfft(filt, axis=0)
    y = jnp.fft.irfft(uf * hf, n=2 * L, axis=1)[:, :L]
    return (y + u * bias).astype(u.dtype)


def hyena(hy, conv_w, filt_params, bias):
    L = hy.shape[1]
    v, x1, x2 = jnp.split(short_conv(hy, conv_w), 3, axis=-1)
    filt = hyena_filters(L, *filt_params)
    bias = bias.astype(jnp.float32)
    z = x1 * long_conv(v, filt[:, 0], bias[0])
    return x2 * long_conv(z, filt[:, 1], bias[1])


def l2norm(a):
    return a * lax.rsqrt(jnp.sum(a * a, axis=-1, keepdims=True) + 1e-6)


def gated_delta_chunked(q, k, v, g, beta):
    b_, t, h, dk = q.shape
    dv = v.shape[-1]
    n = t // DN_CHUNK

    def blocks(a):
        a = a.astype(jnp.float32).reshape((b_, n, DN_CHUNK, h) + a.shape[3:])
        return jnp.moveaxis(a, (1, 3), (0, 2))

    qc, kc, vc, bc = blocks(q), blocks(k), blocks(v), blocks(beta)
    gc = jnp.cumsum(blocks(g), axis=-1)
    causal = jnp.tril(jnp.ones((DN_CHUNK, DN_CHUNK), bool))
    strict = jnp.tril(jnp.ones((DN_CHUNK, DN_CHUNK), bool), -1)
    decay = jnp.exp(jnp.where(causal, gc[..., :, None] - gc[..., None, :], -jnp.inf))
    kb = kc * bc[..., None]
    a_mat = jnp.where(strict, jnp.einsum('nbhid,nbhjd->nbhij', kb, kc) * decay, 0.0)
    t_mat = a_mat + jnp.eye(DN_CHUNK, dtype=jnp.float32)
    u = lax.linalg.triangular_solve(t_mat, vc * bc[..., None], left_side=True, lower=True, unit_diagonal=True)
    w = lax.linalg.triangular_solve(t_mat, kb * jnp.exp(gc)[..., None], left_side=True, lower=True, unit_diagonal=True)
    qk = jnp.einsum('nbhid,nbhjd->nbhij', qc, kc) * decay

    def step(state, xs):
        q_i, k_i, u_i, w_i, g_i, qk_i = xs
        v_new = u_i - jnp.einsum('bhcd,bhde->bhce', w_i, state)
        o = jnp.einsum('bhcd,bhde->bhce', q_i * jnp.exp(g_i)[..., None], state) + jnp.einsum('bhij,bhje->bhie', qk_i, v_new)
        g_last = g_i[..., -1:]
        k_dec = k_i * jnp.exp(g_last - g_i)[..., None]
        state = state * jnp.exp(g_last)[..., None] + jnp.einsum('bhcd,bhce->bhde', k_dec, v_new)
        return state, o

    s0 = jnp.zeros((b_, h, dk, dv), jnp.float32)
    _, o = lax.scan(step, s0, (qc, kc, u, w, gc, qk))
    return jnp.moveaxis(o, (0, 2), (1, 3)).reshape(b_, t, h, dv)


_HI = lax.Precision.HIGHEST


def _dot(a, b):
    return jnp.dot(a, b, precision=_HI, preferred_element_type=jnp.float32)


def _dot_nt(a, b):
    return lax.dot_general(a, b, (((1,), (1,)), ((), ())), precision=_HI, preferred_element_type=jnp.float32)


def _dot_tn(a, b):
    return lax.dot_general(a, b, (((0,), (0,)), ((), ())), precision=_HI, preferred_element_type=jnp.float32)


def _delta_body(q_ref, k_ref, v_ref, gcol_ref, grow_ref, bcol_ref, o_ref, s_ref):
    rev = pl.program_id(0) >= DN_HEADS

    @pl.when(pl.program_id(1) == 0)
    def _():
        s_ref[...] = jnp.zeros_like(s_ref)

    c = DN_CHUNK
    row = lax.broadcasted_iota(jnp.int32, (c, c), 0)
    col = lax.broadcasted_iota(jnp.int32, (c, c), 1)
    lag = jnp.where(rev, col - row, row - col)
    incl = lag >= 0
    strict = lag > 0
    incl_f = incl.astype(jnp.float32)
    eye = (row == col).astype(jnp.float32)

    q = q_ref[...]
    k = k_ref[...]
    v = v_ref[...]
    g_col = gcol_ref[...]
    g_row = grow_ref[...]
    beta = bcol_ref[...]

    gc_col = _dot(incl_f, g_col)
    gc_row = _dot_nt(g_row, incl_f)
    g_tot = jnp.sum(g_col, axis=0, keepdims=True)
    decay = jnp.exp(jnp.where(incl, gc_col - gc_row, -1e30))

    kb = k * beta
    a_mat = jnp.where(strict, _dot_nt(kb, k) * decay, 0.0)
    x = eye - a_mat
    p = _dot(a_mat, a_mat)
    n_sq = c.bit_length() - 2
    for it in range(n_sq):
        x = x + _dot(x, p)
        if it + 1 < n_sq:
            p = _dot(p, p)
    u = _dot(x, v * beta)
    w = _dot(x, kb * jnp.exp(gc_col))
    qk = _dot_nt(q, k) * decay

    s = s_ref[...]
    v_new = u - _dot(w, s)
    o_ref[...] = _dot(q * jnp.exp(gc_col), s) + _dot(qk, v_new)
    k_dec = k * jnp.exp(g_tot - gc_col)
    s_ref[...] = s * jnp.exp(g_tot) + _dot_tn(k_dec, v_new)


def _rev_chunk(cc, n_ctx_chunks, n_chunks):
    return jnp.where(cc < n_ctx_chunks, n_ctx_chunks - 1 - cc, n_chunks + n_ctx_chunks - 1 - cc)


def gated_delta_pallas(q, k, v, g, beta, n_ctx):
    t = q.shape[0]
    nch = t // DN_CHUNK
    ncc = n_ctx // DN_CHUNK
    dh = 2 * DN_HEADS
    g_col = jnp.transpose(g, (0, 2, 1)).reshape(dh, t, 1)
    g_row = g_col.reshape(dh, nch, 1, DN_CHUNK)
    b_col = jnp.transpose(beta, (0, 2, 1)).reshape(dh, t, 1)

    def chunk(i, cc):
        return jnp.where(i >= DN_HEADS, _rev_chunk(cc, ncc, nch), cc)

    qkv_spec = pl.BlockSpec((DN_CHUNK, DN_HEAD_DIM), lambda i, cc: (chunk(i, cc), i % DN_HEADS))
    col_spec = pl.BlockSpec((None, DN_CHUNK, 1), lambda i, cc: (i, chunk(i, cc), 0))
    return pl.pallas_call(
        _delta_body,
        grid=(dh, nch),
        in_specs=[qkv_spec, qkv_spec, qkv_spec, col_spec,
                  pl.BlockSpec((None, None, 1, DN_CHUNK), lambda i, cc: (i, chunk(i, cc), 0, 0)),
                  col_spec],
        out_specs=pl.BlockSpec((None, DN_CHUNK, DN_HEAD_DIM), lambda i, cc: (i // DN_HEADS, chunk(i, cc), i % DN_HEADS)),
        out_shape=jax.ShapeDtypeStruct((2, t, DN_WIDTH), jnp.float32),
        scratch_shapes=[pltpu.VMEM((DN_HEAD_DIM, DN_HEAD_DIM), jnp.float32)],
        compiler_params=pltpu.CompilerParams(dimension_semantics=("arbitrary", "arbitrary")),
        name="gated_delta",
    )(q, k, v, g_col, g_row, b_col)


def gated_head_norm(o, gate, w):
    b_, t_ = gate.shape[:2]
    o = o * lax.rsqrt(jnp.mean(o * o, axis=-1, keepdims=True) + 1e-6) * w.astype(jnp.float32)
    o = o * jax.nn.silu(gate.astype(jnp.float32)).reshape(b_, t_, DN_HEADS, DN_HEAD_DIM)
    return o.reshape(b_, t_, DN_WIDTH).astype(gate.dtype)


def deltanet(dn_ctx, dn_lat, ctx_out, conv_w, a_log, dt_bias, norm_w):
    a_rate = jnp.exp(a_log.astype(jnp.float32))
    dt_bias = dt_bias.astype(jnp.float32)

    def prep(dn):
        b_, t_, _ = dn.shape
        qkv = jax.nn.silu(short_conv(dn[..., :3 * DN_WIDTH], conv_w)).astype(jnp.float32)
        q, k, v = (a.reshape(b_, t_, DN_HEADS, DN_HEAD_DIM) for a in jnp.split(qkv, 3, axis=-1))
        ab = dn[..., 4 * DN_WIDTH:].astype(jnp.float32).reshape(b_, t_, 2, 2, DN_HEADS)
        beta = jax.nn.sigmoid(ab[:, :, 0])
        g = -a_rate * jax.nn.softplus(ab[:, :, 1] + dt_bias)
        return l2norm(q) * DN_HEAD_DIM ** -0.5, l2norm(k), v, beta, g

    pc, pl_ = prep(dn_ctx), prep(dn_lat)
    tc = dn_ctx.shape[1]
    b_ = dn_lat.shape[0]
    assert b_ == 1
    uni = [jnp.concatenate([a, b], axis=1)[0] for a, b in zip(pc, pl_)]
    t_all = uni[0].shape[0]
    o2 = gated_delta_pallas(uni[0].reshape(t_all, DN_WIDTH), uni[1].reshape(t_all, DN_WIDTH), uni[2].reshape(t_all, DN_WIDTH),
                            jnp.transpose(uni[4], (1, 0, 2)), jnp.transpose(uni[3], (1, 0, 2)), tc)
    o = (o2[0] + o2[1]).reshape(1, t_all, DN_HEADS, DN_HEAD_DIM)
    o_ctx, o_lat = o[:, :tc], o[:, tc:]
    y_lat = gated_head_norm(o_lat, dn_lat[..., 3 * DN_WIDTH:4 * DN_WIDTH], norm_w)
    y_ctx = gated_head_norm(o_ctx, dn_ctx[..., 3 * DN_WIDTH:4 * DN_WIDTH], norm_w) if ctx_out else None
    return y_ctx, y_lat


def even_mixer(u_ctx, u_lat, ctx_out, w_in, hy_conv, hy_w1, hy_b1, hy_w2, hy_b2, hy_w3, hy_b3, hy_w4, hy_freq, hy_bias, dn_conv, dn_a_log, dn_dt_bias, dn_norm_w):
    hy_cols = 3 * HY_WIDTH
    n_main = 7 * HY_WIDTH
    filt = (hy_w1, hy_b1, hy_w2, hy_b2, hy_w3, hy_b3, hy_w4, hy_freq)
    w_tail = w_in[:, n_main:]

    def proj_in(u):
        lead = u.shape[:-1]
        a = u.reshape(1, -1, u.shape[-1])
        m = a.shape[1]
        tm = _pick(m, (512, 256))
        tn = 512
        main = pl.pallas_call(
            _mm_body,
            grid=(1, n_main // tn, m // tm),
            in_specs=[
                pl.BlockSpec((None, tm, D_MODEL), lambda ei, j, i: (ei, i, 0)),
                pl.BlockSpec((None, D_MODEL, tn), lambda ei, j, i: (ei, 0, j)),
            ],
            out_specs=pl.BlockSpec((None, tm, tn), lambda ei, j, i: (ei, i, j)),
            out_shape=jax.ShapeDtypeStruct((1, m, n_main), jnp.float32),
            scratch_shapes=[pltpu.VMEM((D_MODEL, tn), jnp.bfloat16)],
            compiler_params=pltpu.CompilerParams(
                dimension_semantics=("arbitrary", "arbitrary", "arbitrary"),
                vmem_limit_bytes=VMEM_LIMIT_BYTES),
            name="proj_in",
        )(a, w_in[None])
        tail = _bmm(a, w_tail[None])
        return jnp.concatenate([main, tail], axis=-1).reshape(lead + (w_in.shape[-1],))

    z_lat = proj_in(u_lat)
    z_ctx = proj_in(u_ctx)
    dn_ctx = z_ctx[..., hy_cols:]
    y_dn_ctx, y_dn_lat = deltanet(dn_ctx, z_lat[..., hy_cols:], ctx_out, dn_conv, dn_a_log, dn_dt_bias, dn_norm_w)
    y_lat = jnp.concatenate([hyena(z_lat[..., :hy_cols], hy_conv, filt, hy_bias), y_dn_lat], axis=-1)
    y_ctx = jnp.concatenate([hyena(z_ctx[..., :hy_cols], hy_conv, filt, hy_bias), y_dn_ctx], axis=-1) if ctx_out else None
    return y_ctx, y_lat


S5_GB = 8
S5_CB = S5_GB * S5_GROUP
S5_LB = S5_GB * S5_STATE
S5_NJ = S5_GROUPS // S5_GB
S5_LANES = S5_GROUPS * S5_STATE
S5_T_TILE = 1056
S5_SCAN_CHUNK = 256
S5_SCAN_ROWS = 32


def _s5_params(lam_re, lam_im, log_dt, b_re, b_im, c_re, c_im):
    f32 = jnp.float32
    lam = lax.complex(jnp.minimum(lam_re.astype(f32), S5_MAX_RE), lam_im.astype(f32))
    lam_bar = jnp.exp(lam * jnp.exp(log_dt.astype(f32))[..., None])
    b_bar = ((lam_bar - 1.0) / lam)[..., None] * lax.complex(b_re.astype(f32), b_im.astype(f32))
    eye = jnp.eye(S5_GB, dtype=f32)

    def w_in(part):
        part = part.reshape(2, S5_NJ, S5_GB, S5_STATE, S5_GROUP)
        return jnp.einsum('gh,djhpc->djgchp', eye, part).reshape(2, S5_NJ, S5_CB, S5_LB)

    def w_out(part):
        part = part.reshape(2, S5_NJ, S5_GB, S5_GROUP, S5_STATE)
        return jnp.einsum('gh,djhcp->djhpgc', eye, part).reshape(2, S5_NJ, S5_LB, S5_CB)

    wi_re, wi_im = w_in(jnp.real(b_bar)), w_in(jnp.imag(b_bar))
    w1 = jnp.stack([wi_re[0], wi_im[0], wi_re[1], wi_im[1]], axis=2).reshape(S5_NJ, S5_CB, 4 * S5_LB)
    wo_re, wo_im = w_out(c_re.astype(f32)), w_out(-c_im.astype(f32))
    w3 = jnp.stack([wo_re[0], wo_im[0], wo_re[1], wo_im[1]], axis=1)
    a_re = jnp.real(lam_bar).reshape(2, S5_LANES // 128, 128)
    a_im = jnp.imag(lam_bar).reshape(2, S5_LANES // 128, 128)
    return w1, w3, a_re, a_im


def _s5_bu_body(u_ref, w_ref, o_ref):
    y = jnp.dot(u_ref[...].astype(jnp.bfloat16), w_ref[...].astype(jnp.bfloat16), preferred_element_type=jnp.float32)
    for q in range(4):
        o_ref[q] = y[:, q * S5_LB:(q + 1) * S5_LB]


def _s5_scan_body(bre_ref, bim_ref, are_ref, aim_ref, xre_ref, xim_ref, cre_ref, cim_ref):
    rev = pl.program_id(0) == 1

    @pl.when(pl.program_id(2) == 0)
    def _():
        cre_ref[...] = jnp.zeros_like(cre_ref)
        cim_ref[...] = jnp.zeros_like(cim_ref)

    ar = are_ref[...]
    ai = aim_ref[...]

    def step(i, carry):
        xr, xi = carry
        t = jnp.where(rev, S5_SCAN_CHUNK - 1 - i, i)
        nr = ar * xr - ai * xi + bre_ref[t]
        ni = ar * xi + ai * xr + bim_ref[t]
        xre_ref[t] = nr
        xim_ref[t] = ni
        return nr, ni

    xr, xi = lax.fori_loop(0, S5_SCAN_CHUNK, step, (cre_ref[...], cim_ref[...]), unroll=4)
    cre_ref[...] = xr
    cim_ref[...] = xi


def _s5_out_body(xre_ref, xim_ref, u_ref, w_ref, d_ref, o_ref):
    acc = None
    for d in range(2):
        for x_ref, q in ((xre_ref, 2 * d), (xim_ref, 2 * d + 1)):
            part = jnp.dot(x_ref[d].astype(jnp.bfloat16), w_ref[q].astype(jnp.bfloat16), preferred_element_type=jnp.float32)
            acc = part if acc is None else acc + part
    o_ref[...] = jax.nn.gelu(acc + d_ref[...] * u_ref[...])


def s5_pallas(u, n_ctx, lam_re, lam_im, log_dt, b_re, b_im, c_re, c_im, d_skip):
    t, d = u.shape
    w1, w3, a_re, a_im = _s5_params(lam_re, lam_im, log_dt, b_re, b_im, c_re, c_im)
    nt = t // S5_T_TILE
    bu = pl.pallas_call(
        _s5_bu_body,
        grid=(S5_NJ, nt),
        in_specs=[pl.BlockSpec((S5_T_TILE, S5_CB), lambda j, i: (i, j)),
                  pl.BlockSpec((None, S5_CB, 4 * S5_LB), lambda j, i: (j, 0, 0))],
        out_specs=pl.BlockSpec((4, S5_T_TILE, S5_LB), lambda j, i: (0, i, j)),
        out_shape=jax.ShapeDtypeStruct((4, t, S5_LANES), jnp.float32),
        compiler_params=pltpu.CompilerParams(dimension_semantics=("arbitrary", "arbitrary"), vmem_limit_bytes=VMEM_LIMIT_BYTES),
        name="s5_bu",
    )(u, w1)

    assert n_ctx == S5_SCAN_CHUNK
    nch = t // S5_SCAN_CHUNK
    rows = S5_LANES // 128
    bu4 = bu.reshape(4, t, rows, 128)

    def tchunk(dd, cc):
        return jnp.where(dd == 1, _rev_chunk(cc, 1, nch), cc)

    def plane_spec(plane):
        return pl.BlockSpec((None, S5_SCAN_CHUNK, S5_SCAN_ROWS, 128), lambda dd, sg, cc: (2 * dd + plane, tchunk(dd, cc), sg, 0))

    a_spec = pl.BlockSpec((None, S5_SCAN_ROWS, 128), lambda dd, sg, cc: (dd, sg, 0))
    x_spec = pl.BlockSpec((None, S5_SCAN_CHUNK, S5_SCAN_ROWS, 128), lambda dd, sg, cc: (dd, tchunk(dd, cc), sg, 0))
    x_shape = jax.ShapeDtypeStruct((2, t, rows, 128), jnp.float32)
    x_re, x_im = pl.pallas_call(
        _s5_scan_body,
        grid=(2, rows // S5_SCAN_ROWS, nch),
        in_specs=[plane_spec(0), plane_spec(1), a_spec, a_spec],
        out_specs=[x_spec, x_spec],
        out_shape=[x_shape, x_shape],
        scratch_shapes=[pltpu.VMEM((S5_SCAN_ROWS, 128), jnp.float32), pltpu.VMEM((S5_SCAN_ROWS, 128), jnp.float32)],
        compiler_params=pltpu.CompilerParams(dimension_semantics=("arbitrary", "arbitrary", "arbitrary"), vmem_limit_bytes=VMEM_LIMIT_BYTES),
        name="s5_scan",
    )(bu4, bu4, a_re, a_im)

    xs_spec = pl.BlockSpec((2, S5_T_TILE, S5_LB), lambda j, i: (0, i, j))
    return pl.pallas_call(
        _s5_out_body,
        grid=(S5_NJ, nt),
        in_specs=[xs_spec, xs_spec,
                  pl.BlockSpec((S5_T_TILE, S5_CB), lambda j, i: (i, j)),
                  pl.BlockSpec((None, 4, S5_LB, S5_CB), lambda j, i: (j, 0, 0, 0)),
                  pl.BlockSpec((1, S5_CB), lambda j, i: (0, j))],
        out_specs=pl.BlockSpec((S5_T_TILE, S5_CB), lambda j, i: (i, j)),
        out_shape=jax.ShapeDtypeStruct((t, d), jnp.float32),
        compiler_params=pltpu.CompilerParams(dimension_semantics=("arbitrary", "arbitrary"), vmem_limit_bytes=VMEM_LIMIT_BYTES),
        name="s5_out",
    )(x_re.reshape(2, t, S5_LANES), x_im.reshape(2, t, S5_LANES), u, w3, d_skip.astype(jnp.float32).reshape(1, d))


def s5_mixer(u_ctx, u_lat, ctx_out, lam_re, lam_im, log_dt, b_re, b_im, c_re, c_im, d_skip):
    b_, tc, d = u_ctx.shape
    assert b_ == 1
    u = jnp.concatenate([u_ctx[0], u_lat[0]], axis=0).astype(jnp.float32)
    f = s5_pallas(u, tc, lam_re, lam_im, log_dt, b_re, b_im, c_re, c_im, d_skip)
    f_lat = f[None, tc:]
    f_ctx = f[None, :tc] if ctx_out else None
    return f_ctx, f_lat


def glu_out(f, w):
    gv = _mm(f, w)
    return gv[..., :D_MODEL] * jax.nn.sigmoid(gv[..., D_MODEL:])


def expert_choice_ffn(h, w_router, w_in, w_out):
    b_, n, d = h.shape
    cap = EC_CAPACITY * n // N_EXPERTS
    aff = jax.nn.softmax(jnp.einsum('bnd,de->bne', h, w_router).astype(jnp.float32), axis=-1)
    gate, idx = lax.top_k(jnp.swapaxes(aff, 1, 2), cap)
    xs = jax.vmap(lambda hb, ib: hb[ib])(h, idx)
    gu = _bmm(xs[0], w_in)
    act = jax.nn.silu(gu[..., :EXPERT_FF]) * gu[..., EXPERT_FF:]
    out = (_bmm(act, w_out) * gate[0][..., None].astype(h.dtype))[None]
    return jax.vmap(lambda ib, ob: jnp.zeros((n, d), ob.dtype).at[ib.reshape(-1)].add(ob.reshape(-1, d)))(idx, out)


def kernel(x, c, ctx, c_ctx, ada_w, ada_b, ln_g, ln_b, ev_w_in, ev_w_out, hy_conv, hy_w1, hy_b1, hy_w2, hy_b2, hy_w3, hy_b3, hy_w4, hy_freq, hy_bias, dn_conv, dn_a_log, dn_dt_bias, dn_norm_w, s5_lam_re, s5_lam_im, s5_log_dt, s5_b_re, s5_b_im, s5_c_re, s5_c_im, s5_d, od_w_glu, moe_router, moe_w_in, moe_w_out):
    d = D_MODEL
    h_lat, h_ctx = x, ctx
    s_lat = jax.nn.silu(c)
    s_ctx = jax.nn.silu(c_ctx)[None]
    for l in range(DEPTH):
        last = l == DEPTH - 1
        col = (l // 2) % 2 == 1
        i = l // 2
        n_ctx_mod = 2 if last else 6
        mod_lat = jnp.split((s_lat @ ada_w[l] + ada_b[l])[:, None], 6, axis=-1)
        mod_ctx = jnp.split((s_ctx @ ada_w[l][:, :n_ctx_mod * d] + ada_b[l][:n_ctx_mod * d])[:, None], n_ctx_mod, axis=-1)
        u_lat = h_lat * (1.0 + mod_lat[1]) + mod_lat[0]
        u_ctx = h_ctx * (1.0 + mod_ctx[1]) + mod_ctx[0]
        if col:
            u_lat = to_col_major(u_lat)
        if l % 2 == 0:
            f_ctx, f_lat = even_mixer(u_ctx, u_lat, not last, ev_w_in[i], hy_conv[i], hy_w1[i], hy_b1[i], hy_w2[i], hy_b2[i], hy_w3[i], hy_b3[i], hy_w4[i], hy_freq[i], hy_bias[i], dn_conv[i], dn_a_log[i], dn_dt_bias[i], dn_norm_w[i])
            w_proj = ev_w_out[i]
            proj = _mm
        else:
            f_ctx, f_lat = s5_mixer(u_ctx, u_lat, not last, s5_lam_re[i], s5_lam_im[i], s5_log_dt[i], s5_b_re[i], s5_b_im[i], s5_c_re[i], s5_c_im[i], s5_d[i])
            w_proj = od_w_glu[i]
            proj = glu_out
        if col:
            f_lat = from_col_major(f_lat)
        h_lat = post_norm(h_lat, mod_lat[2] * proj(f_lat, w_proj), ln_g[l, 0], ln_b[l, 0])
        m_lat = expert_choice_ffn(h_lat * (1.0 + mod_lat[4]) + mod_lat[3], moe_router[l], moe_w_in[l], moe_w_out[l])
        h_lat = post_norm(h_lat, mod_lat[5] * m_lat, ln_g[l, 1], ln_b[l, 1])
        if not last:
            h_ctx = post_norm(h_ctx, mod_ctx[2] * proj(f_ctx, w_proj), ln_g[l, 0], ln_b[l, 0])
            m_ctx = expert_choice_ffn(h_ctx * (1.0 + mod_ctx[4]) + mod_ctx[3], moe_router[l], moe_w_in[l], moe_w_out[l])
            h_ctx = post_norm(h_ctx, mod_ctx[5] * m_ctx, ln_g[l, 1], ln_b[l, 1])
    return h_lat
```

```python
import functools
import math

import jax
import jax.numpy as jnp
from jax import lax
from jax.experimental import pallas as pl
from jax.experimental.pallas import tpu as pltpu

D_MODEL = 4096
DEPTH = 4
CTX_LEN = 256
GRID_W = 64
HY_WIDTH = D_MODEL // 2
HY_ORDER = 2
HY_EMB = 33
HY_BANDS = (HY_EMB - 1) // 2
HY_TARGET = 1e-2
HY_MAX_DECAY = math.log(HY_TARGET) / 0.3
HY_MIN_DECAY = math.log(HY_TARGET) / 1.5
DN_HEADS = 16
DN_HEAD_DIM = 128
DN_WIDTH = DN_HEADS * DN_HEAD_DIM
DN_CHUNK = 64
S5_GROUP = 16
S5_GROUPS = D_MODEL // S5_GROUP
S5_STATE = 64
S5_BLOCK = 32
S5_MAX_RE = -1e-4
N_EXPERTS = 16
EXPERT_FF = 384
EC_CAPACITY = 2
DEEPNORM_ALPHA = (2 * DEPTH) ** 0.25
LN_EPS = 1e-5

VMEM_LIMIT_BYTES = 56 * 1024 * 1024


def _mm_body(a_ref, w_ref, o_ref, wb_ref):
    @pl.when(pl.program_id(2) == 0)
    def _():
        wb_ref[...] = w_ref[...].astype(jnp.bfloat16)

    o_ref[...] = jnp.dot(a_ref[...].astype(jnp.bfloat16), wb_ref[...], preferred_element_type=jnp.float32).astype(o_ref.dtype)


def _pick(n, cands):
    for c in cands:
        if n % c == 0:
            return c
    return n


def _bmm(a, w, out_dtype=jnp.float32):
    e, m, k = a.shape
    n = w.shape[-1]
    tm = _pick(m, (512, 256, 128, 64, 32, 16, 8))
    tn = _pick(n, (512, 384, 256, 128))
    return pl.pallas_call(
        _mm_body,
        grid=(e, n // tn, m // tm),
        in_specs=[
            pl.BlockSpec((None, tm, k), lambda ei, j, i: (ei, i, 0)),
            pl.BlockSpec((None, k, tn), lambda ei, j, i: (ei, 0, j)),
        ],
        out_specs=pl.BlockSpec((None, tm, tn), lambda ei, j, i: (ei, i, j)),
        out_shape=jax.ShapeDtypeStruct((e, m, n), out_dtype),
        scratch_shapes=[pltpu.VMEM((k, tn), jnp.bfloat16)],
        compiler_params=pltpu.CompilerParams(
            dimension_semantics=("arbitrary", "arbitrary", "arbitrary"),
            vmem_limit_bytes=VMEM_LIMIT_BYTES),
        name="bmm",
    )(a, w)


def _mm(a, w):
    lead = a.shape[:-1]
    out = _bmm(a.reshape(1, -1, a.shape[-1]), w[None])
    return out.reshape(lead + (w.shape[-1],))


def post_norm(h, y, g, b):
    z = (DEEPNORM_ALPHA * h + y).astype(jnp.float32)
    mu = jnp.mean(z, axis=-1, keepdims=True)
    var = jnp.mean(jnp.square(z - mu), axis=-1, keepdims=True)
    return ((z - mu) * lax.rsqrt(var + LN_EPS) * g.astype(jnp.float32) + b.astype(jnp.float32)).astype(h.dtype)


def short_conv(u, w):
    k, ch = w.shape
    return lax.conv_general_dilated(u, w.astype(u.dtype)[:, None, :], window_strides=(1,), padding=[(k // 2, k // 2)], dimension_numbers=('NWC', 'WIO', 'NWC'), feature_group_count=ch)


def cat_streams(a_ctx, a_lat, rev):
    if rev:
        return jnp.concatenate([a_ctx[:, ::-1], a_lat[:, ::-1]], axis=1)
    return jnp.concatenate([a_ctx, a_lat], axis=1)


def to_col_major(h):
    b_, n, d = h.shape
    rows = n // GRID_W
    return h.reshape(b_, rows, GRID_W, d).transpose(0, 2, 1, 3).reshape(b_, n, d)


def from_col_major(h):
    b_, n, d = h.shape
    rows = n // GRID_W
    return h.reshape(b_, GRID_W, rows, d).transpose(0, 2, 1, 3).reshape(b_, n, d)


def hyena_filters(L, w1, b1, w2, b2, w3, b3, w4, freq):
    f32 = jnp.float32
    bands = jnp.linspace(1e-4, HY_BANDS - 1, HY_BANDS, dtype=f32)
    freq = freq.astype(f32)
    deltas = jnp.abs(jnp.linspace(HY_MIN_DECAY, HY_MAX_DECAY, HY_WIDTH, dtype=f32))
    w4d = w4.astype(f32).reshape(w4.shape[0], HY_ORDER, 2, HY_WIDTH)

    def taps(pos, direction):
        t = pos / max(L - 1, 1)
        ang = (2.0 * math.pi / L) * pos
        z = jnp.concatenate([t[:, None], jnp.cos(ang[:, None] * bands), -jnp.sin(ang[:, None] * bands)], axis=-1)
        h = jnp.sin(freq[0] * (z @ w1.astype(f32) + b1.astype(f32)))
        h = jnp.sin(freq[1] * (h @ w2.astype(f32) + b2.astype(f32)))
        h = jnp.sin(freq[2] * (h @ w3.astype(f32) + b3.astype(f32)))
        h = (h @ w4d[:, :, direction].reshape(w4.shape[0], HY_ORDER * HY_WIDTH)).reshape(pos.shape[0], HY_ORDER, HY_WIDTH)
        return h * jnp.exp(-t[:, None, None] * deltas)

    fwd = taps(jnp.arange(L, dtype=f32), 0)
    bwd_rev = taps(jnp.arange(L - 1, 0, -1, dtype=f32), 1)
    return jnp.concatenate([fwd, jnp.zeros_like(fwd[:1]), bwd_rev], axis=0)


def long_conv(u, filt, bias):
    L = u.shape[1]
    uf = jnp.fft.rfft(u.astype(jnp.float32), n=2 * L, axis=1)
    hf = jnp.fft.rfft(filt, axis=0)
    y = jnp.fft.irfft(uf * hf, n=2 * L, axis=1)[:, :L]
    return (y + u * bias).astype(u.dtype)


def hyena(hy, conv_w, filt_params, bias):
    L = hy.shape[1]
    v, x1, x2 = jnp.split(short_conv(hy, conv_w), 3, axis=-1)
    filt = hyena_filters(L, *filt_params)
    bias = bias.astype(jnp.float32)
    z = x1 * long_conv(v, filt[:, 0], bias[0])
    return x2 * long_conv(z, filt[:, 1], bias[1])


def l2norm(a):
    return a * lax.rsqrt(jnp.sum(a * a, axis=-1, keepdims=True) + 1e-6)


def gated_delta_chunked(q, k, v, g, beta):
    b_, t, h, dk = q.shape
    dv = v.shape[-1]
    n = t // DN_CHUNK

    def blocks(a):
        a = a.astype(jnp.float32).reshape((b_, n, DN_CHUNK, h) + a.shape[3:])
        return jnp.moveaxis(a, (1, 3), (0, 2))

    qc, kc, vc, bc = blocks(q), blocks(k), blocks(v), blocks(beta)
    gc = jnp.cumsum(blocks(g), axis=-1)
    causal = jnp.tril(jnp.ones((DN_CHUNK, DN_CHUNK), bool))
    strict = jnp.tril(jnp.ones((DN_CHUNK, DN_CHUNK), bool), -1)
    decay = jnp.exp(jnp.where(causal, gc[..., :, None] - gc[..., None, :], -jnp.inf))
    kb = kc * bc[..., None]
    a_mat = jnp.where(strict, jnp.einsum('nbhid,nbhjd->nbhij', kb, kc) * decay, 0.0)
    t_mat = a_mat + jnp.eye(DN_CHUNK, dtype=jnp.float32)
    u = lax.linalg.triangular_solve(t_mat, vc * bc[..., None], left_side=True, lower=True, unit_diagonal=True)
    w = lax.linalg.triangular_solve(t_mat, kb * jnp.exp(gc)[..., None], left_side=True, lower=True, unit_diagonal=True)
    qk = jnp.einsum('nbhid,nbhjd->nbhij', qc, kc) * decay

    def step(state, xs):
        q_i, k_i, u_i, w_i, g_i, qk_i = xs
        v_new = u_i - jnp.einsum('bhcd,bhde->bhce', w_i, state)
        o = jnp.einsum('bhcd,bhde->bhce', q_i * jnp.exp(g_i)[..., None], state) + jnp.einsum('bhij,bhje->bhie', qk_i, v_new)
        g_last = g_i[..., -1:]
        k_dec = k_i * jnp.exp(g_last - g_i)[..., None]
        state = state * jnp.exp(g_last)[..., None] + jnp.einsum('bhcd,bhce->bhde', k_dec, v_new)
        return state, o

    s0 = jnp.zeros((b_, h, dk, dv), jnp.float32)
    _, o = lax.scan(step, s0, (qc, kc, u, w, gc, qk))
    return jnp.moveaxis(o, (0, 2), (1, 3)).reshape(b_, t, h, dv)


def _split_bf16(a):
    hi = a.astype(jnp.bfloat16)
    lo = (a - hi.astype(jnp.float32)).astype(jnp.bfloat16)
    return hi, lo


def _dot3(a, b, dims):
    ah, al = _split_bf16(a)
    bh, bl = _split_bf16(b)
    dn = (dims, ((), ()))

    def dg(x, y):
        return lax.dot_general(x, y, dn, preferred_element_type=jnp.float32)

    return dg(ah, bh) + (dg(ah, bl) + dg(al, bh))


def _dot1(a, b, dims):
    return lax.dot_general(a.astype(jnp.bfloat16), b.astype(jnp.bfloat16), (dims, ((), ())), preferred_element_type=jnp.float32)


def _dot(a, b):
    return _dot3(a, b, ((1,), (0,)))


def _dot_nt(a, b):
    return _dot3(a, b, ((1,), (1,)))


def _dot_tn(a, b):
    return _dot3(a, b, ((0,), (0,)))


DN_HB = 4
DN_HG = DN_HEADS // DN_HB


def _delta_body(q_ref, k_ref, v_ref, gcol_ref, grow_ref, bcol_ref, o_ref, s_ref):
    rev = pl.program_id(0) >= DN_HG

    @pl.when(pl.program_id(1) == 0)
    def _():
        s_ref[...] = jnp.zeros_like(s_ref)

    c = DN_CHUNK
    row = lax.broadcasted_iota(jnp.int32, (c, c), 0)
    col = lax.broadcasted_iota(jnp.int32, (c, c), 1)
    lag = jnp.where(rev, col - row, row - col)
    incl = lag >= 0
    strict = lag > 0
    incl_f = incl.astype(jnp.float32)
    eye = (row == col).astype(jnp.float32)
    n_sq = c.bit_length() - 2

    for hb in range(DN_HB):
        lanes = pl.ds(hb * DN_HEAD_DIM, DN_HEAD_DIM)
        q = q_ref[:, lanes]
        k = k_ref[:, lanes]
        v = v_ref[:, lanes]
        g_col = gcol_ref[hb]
        g_row = grow_ref[hb]
        beta = bcol_ref[hb]

        gc_col = _dot(incl_f, g_col)
        gc_row = _dot_nt(g_row, incl_f)
        g_tot = jnp.sum(g_col, axis=0, keepdims=True)
        decay = jnp.exp(jnp.where(incl, gc_col - gc_row, -1e30))

        kb = k * beta
        a_mat = jnp.where(strict, _dot_nt(kb, k) * decay, 0.0)
        x = eye - a_mat
        p = _dot(a_mat, a_mat)
        for it in range(n_sq):
            x = x + _dot(x, p)
            if it + 1 < n_sq:
                p = _dot(p, p)
        u = _dot(x, v * beta)
        w = _dot(x, kb * jnp.exp(gc_col))
        qk = _dot1(q, k, ((1,), (1,))) * decay

        s = s_ref[hb]
        v_new = u - _dot(w, s)
        o_ref[:, lanes] = _dot1(q * jnp.exp(gc_col), s, ((1,), (0,))) + _dot1(qk, v_new, ((1,), (0,)))
        k_dec = k * jnp.exp(g_tot - gc_col)
        s_ref[hb] = s * jnp.exp(g_tot) + _dot_tn(k_dec, v_new)


def _rev_chunk(cc, n_ctx_chunks, n_chunks):
    return jnp.where(cc < n_ctx_chunks, n_ctx_chunks - 1 - cc, n_chunks + n_ctx_chunks - 1 - cc)


def gated_delta_pallas(q, k, v, g, beta, n_ctx):
    t = q.shape[0]
    nch = t // DN_CHUNK
    ncc = n_ctx // DN_CHUNK
    dh = 2 * DN_HEADS
    g_col = jnp.transpose(g, (0, 2, 1)).reshape(dh, t, 1)
    g_row = g_col.reshape(dh, nch, 1, DN_CHUNK)
    b_col = jnp.transpose(beta, (0, 2, 1)).reshape(dh, t, 1)

    def chunk(i, cc):
        return jnp.where(i >= DN_HG, _rev_chunk(cc, ncc, nch), cc)

    qkv_spec = pl.BlockSpec((DN_CHUNK, DN_HB * DN_HEAD_DIM), lambda i, cc: (chunk(i, cc), i % DN_HG))
    col_spec = pl.BlockSpec((DN_HB, DN_CHUNK, 1), lambda i, cc: (i, chunk(i, cc), 0))
    return pl.pallas_call(
        _delta_body,
        grid=(2 * DN_HG, nch),
        in_specs=[qkv_spec, qkv_spec, qkv_spec, col_spec,
                  pl.BlockSpec((DN_HB, None, 1, DN_CHUNK), lambda i, cc: (i, chunk(i, cc), 0, 0)),
                  col_spec],
        out_specs=pl.BlockSpec((None, DN_CHUNK, DN_HB * DN_HEAD_DIM), lambda i, cc: (i // DN_HG, chunk(i, cc), i % DN_HG)),
        out_shape=jax.ShapeDtypeStruct((2, t, DN_WIDTH), jnp.float32),
        scratch_shapes=[pltpu.VMEM((DN_HB, DN_HEAD_DIM, DN_HEAD_DIM), jnp.float32)],
        compiler_params=pltpu.CompilerParams(dimension_semantics=("arbitrary", "arbitrary")),
        name="gated_delta",
    )(q, k, v, g_col, g_row, b_col)


def gated_head_norm(o, gate, w):
    b_, t_ = gate.shape[:2]
    o = o * lax.rsqrt(jnp.mean(o * o, axis=-1, keepdims=True) + 1e-6) * w.astype(jnp.float32)
    o = o * jax.nn.silu(gate.astype(jnp.float32)).reshape(b_, t_, DN_HEADS, DN_HEAD_DIM)
    return o.reshape(b_, t_, DN_WIDTH).astype(gate.dtype)


def deltanet(dn_ctx, dn_lat, ctx_out, conv_w, a_log, dt_bias, norm_w):
    a_rate = jnp.exp(a_log.astype(jnp.float32))
    dt_bias = dt_bias.astype(jnp.float32)

    def prep(dn):
        b_, t_, _ = dn.shape
        qkv = jax.nn.silu(short_conv(dn[..., :3 * DN_WIDTH], conv_w)).astype(jnp.float32)
        q, k, v = (a.reshape(b_, t_, DN_HEADS, DN_HEAD_DIM) for a in jnp.split(qkv, 3, axis=-1))
        ab = dn[..., 4 * DN_WIDTH:].astype(jnp.float32).reshape(b_, t_, 2, 2, DN_HEADS)
        beta = jax.nn.sigmoid(ab[:, :, 0])
        g = -a_rate * jax.nn.softplus(ab[:, :, 1] + dt_bias)
        return l2norm(q) * DN_HEAD_DIM ** -0.5, l2norm(k), v, beta, g

    pc, pl_ = prep(dn_ctx), prep(dn_lat)
    tc = dn_ctx.shape[1]
    b_ = dn_lat.shape[0]
    assert b_ == 1
    uni = [jnp.concatenate([a, b], axis=1)[0] for a, b in zip(pc, pl_)]
    t_all = uni[0].shape[0]
    o2 = gated_delta_pallas(uni[0].reshape(t_all, DN_WIDTH), uni[1].reshape(t_all, DN_WIDTH), uni[2].reshape(t_all, DN_WIDTH),
                            jnp.transpose(uni[4], (1, 0, 2)), jnp.transpose(uni[3], (1, 0, 2)), tc)
    o = (o2[0] + o2[1]).reshape(1, t_all, DN_HEADS, DN_HEAD_DIM)
    o_ctx, o_lat = o[:, :tc], o[:, tc:]
    y_lat = gated_head_norm(o_lat, dn_lat[..., 3 * DN_WIDTH:4 * DN_WIDTH], norm_w)
    y_ctx = gated_head_norm(o_ctx, dn_ctx[..., 3 * DN_WIDTH:4 * DN_WIDTH], norm_w) if ctx_out else None
    return y_ctx, y_lat


def even_mixer(u_ctx, u_lat, ctx_out, w_in, hy_conv, hy_w1, hy_b1, hy_w2, hy_b2, hy_w3, hy_b3, hy_w4, hy_freq, hy_bias, dn_conv, dn_a_log, dn_dt_bias, dn_norm_w):
    hy_cols = 3 * HY_WIDTH
    n_main = 7 * HY_WIDTH
    filt = (hy_w1, hy_b1, hy_w2, hy_b2, hy_w3, hy_b3, hy_w4, hy_freq)
    w_tail = w_in[:, n_main:]

    def proj_in(u):
        lead = u.shape[:-1]
        a = u.reshape(1, -1, u.shape[-1])
        m = a.shape[1]
        tm = _pick(m, (512, 256))
        tn = 512
        main = pl.pallas_call(
            _mm_body,
            grid=(1, n_main // tn, m // tm),
            in_specs=[
                pl.BlockSpec((None, tm, D_MODEL), lambda ei, j, i: (ei, i, 0)),
                pl.BlockSpec((None, D_MODEL, tn), lambda ei, j, i: (ei, 0, j)),
            ],
            out_specs=pl.BlockSpec((None, tm, tn), lambda ei, j, i: (ei, i, j)),
            out_shape=jax.ShapeDtypeStruct((1, m, n_main), jnp.float32),
            scratch_shapes=[pltpu.VMEM((D_MODEL, tn), jnp.bfloat16)],
            compiler_params=pltpu.CompilerParams(
                dimension_semantics=("arbitrary", "arbitrary", "arbitrary"),
                vmem_limit_bytes=VMEM_LIMIT_BYTES),
            name="proj_in",
        )(a, w_in[None])
        tail = _bmm(a, w_tail[None])
        return jnp.concatenate([main, tail], axis=-1).reshape(lead + (w_in.shape[-1],))

    z_lat = proj_in(u_lat)
    z_ctx = proj_in(u_ctx)
    dn_ctx = z_ctx[..., hy_cols:]
    y_dn_ctx, y_dn_lat = deltanet(dn_ctx, z_lat[..., hy_cols:], ctx_out, dn_conv, dn_a_log, dn_dt_bias, dn_norm_w)
    y_lat = jnp.concatenate([hyena(z_lat[..., :hy_cols], hy_conv, filt, hy_bias), y_dn_lat], axis=-1)
    y_ctx = jnp.concatenate([hyena(z_ctx[..., :hy_cols], hy_conv, filt, hy_bias), y_dn_ctx], axis=-1) if ctx_out else None
    return y_ctx, y_lat


S5_GB = 8
S5_CB = S5_GB * S5_GROUP
S5_LB = S5_GB * S5_STATE
S5_NJ = S5_GROUPS // S5_GB
S5_LANES = S5_GROUPS * S5_STATE
S5_T_TILE = 1056
S5_SCAN_CHUNK = 256
S5_SCAN_LANES = 1024


def _s5_params(lam_re, lam_im, log_dt, b_re, b_im, c_re, c_im):
    f32 = jnp.float32
    lam = lax.complex(jnp.minimum(lam_re.astype(f32), S5_MAX_RE), lam_im.astype(f32))
    lam_bar = jnp.exp(lam * jnp.exp(log_dt.astype(f32))[..., None])
    b_bar = ((lam_bar - 1.0) / lam)[..., None] * lax.complex(b_re.astype(f32), b_im.astype(f32))
    eye = jnp.eye(S5_GB, dtype=f32)

    def w_in(part):
        part = part.reshape(2, S5_NJ, S5_GB, S5_STATE, S5_GROUP)
        return jnp.einsum('gh,djhpc->djgchp', eye, part).reshape(2, S5_NJ, S5_CB, S5_LB)

    def w_out(part):
        part = part.reshape(2, S5_NJ, S5_GB, S5_GROUP, S5_STATE)
        return jnp.einsum('gh,djhcp->djhpgc', eye, part).reshape(2, S5_NJ, S5_LB, S5_CB)

    wi_re, wi_im = w_in(jnp.real(b_bar)), w_in(jnp.imag(b_bar))
    w1 = jnp.stack([wi_re[0], wi_im[0], wi_re[1], wi_im[1]], axis=2).reshape(S5_NJ, S5_CB, 4 * S5_LB)
    wo_re, wo_im = w_out(c_re.astype(f32)), w_out(-c_im.astype(f32))
    w3 = jnp.stack([wo_re[0], wo_im[0], wo_re[1], wo_im[1]], axis=1)
    a = lam_bar.reshape(2, 1, S5_LANES)
    pows = [a]
    for _ in range(7):
        pows.append(pows[-1] * a)
    pows = jnp.concatenate(pows, axis=1)
    r = jnp.arange(8)[None, :, None]
    done = jnp.stack([r, 7 - r])
    done = done.reshape(2, 8, 1)
    tabs = []
    for sh in (1, 2, 4):
        tabs.append(jnp.where(done >= sh, pows[:, sh - 1:sh], 0.0))
    tabs.append(jnp.stack([pows[0], pows[1, ::-1]]))
    coef = jnp.stack([part for tab in tabs for part in (jnp.real(tab), jnp.imag(tab))], axis=1)
    return w1, w3, coef


def _s5_bu_body(u_ref, w_ref, o_ref):
    y = jnp.dot(u_ref[...].astype(jnp.bfloat16), w_ref[...].astype(jnp.bfloat16), preferred_element_type=jnp.float32)
    for q in range(4):
        o_ref[q] = y[:, q * S5_LB:(q + 1) * S5_LB]


def _s5_scan_body(bre_ref, bim_ref, coef_ref, xre_ref, xim_ref, cre_ref, cim_ref):

    @pl.when(pl.program_id(2) == 0)
    def _():
        cre_ref[...] = jnp.zeros_like(cre_ref)
        cim_ref[...] = jnp.zeros_like(cim_ref)

    n_groups = S5_SCAN_CHUNK // 8

    def run(rev):
        br = bre_ref[...]
        bi = bim_ref[...]
        for n, sh in enumerate((1, 2, 4)):
            are = coef_ref[2 * n][None]
            aim = coef_ref[2 * n + 1][None]
            shift = 8 - sh if rev else sh
            rr = pltpu.roll(br, shift, 1)
            ri = pltpu.roll(bi, shift, 1)
            br, bi = br + are * rr - aim * ri, bi + are * ri + aim * rr
        pre = coef_ref[6]
        pim = coef_ref[7]
        cr = cre_ref[...]
        ci = cim_ref[...]
        last = 0 if rev else 7
        for g in (range(n_groups - 1, -1, -1) if rev else range(n_groups)):
            xr = br[g] + pre * cr - pim * ci
            xi = bi[g] + pre * ci + pim * cr
            xre_ref[g] = xr
            xim_ref[g] = xi
            cr = xr[last:last + 1]
            ci = xi[last:last + 1]
        cre_ref[...] = cr
        cim_ref[...] = ci

    @pl.when(pl.program_id(0) == 0)
    def _():
        run(False)

    @pl.when(pl.program_id(0) == 1)
    def _():
        run(True)


def _s5_out_body(xre_ref, xim_ref, u_ref, w_ref, d_ref, o_ref):
    acc = None
    for d in range(2):
        for x_ref, q in ((xre_ref, 2 * d), (xim_ref, 2 * d + 1)):
            part = jnp.dot(x_ref[d].astype(jnp.bfloat16), w_ref[q].astype(jnp.bfloat16), preferred_element_type=jnp.float32)
            acc = part if acc is None else acc + part
    o_ref[...] = jax.nn.gelu(acc + d_ref[...] * u_ref[...])


def s5_pallas(u, n_ctx, lam_re, lam_im, log_dt, b_re, b_im, c_re, c_im, d_skip):
    t, d = u.shape
    w1, w3, coef = _s5_params(lam_re, lam_im, log_dt, b_re, b_im, c_re, c_im)
    nt = t // S5_T_TILE
    bu = pl.pallas_call(
        _s5_bu_body,
        grid=(S5_NJ, nt),
        in_specs=[pl.BlockSpec((S5_T_TILE, S5_CB), lambda j, i: (i, j)),
                  pl.BlockSpec((None, S5_CB, 4 * S5_LB), lambda j, i: (j, 0, 0))],
        out_specs=pl.BlockSpec((4, S5_T_TILE, S5_LB), lambda j, i: (0, i, j)),
        out_shape=jax.ShapeDtypeStruct((4, t, S5_LANES), jnp.float32),
        compiler_params=pltpu.CompilerParams(dimension_semantics=("arbitrary", "arbitrary"), vmem_limit_bytes=VMEM_LIMIT_BYTES),
        name="s5_bu",
    )(u, w1)

    assert n_ctx == S5_SCAN_CHUNK
    nch = t // S5_SCAN_CHUNK
    n_groups = S5_SCAN_CHUNK // 8
    bu4 = bu.reshape(4, t // 8, 8, S5_LANES)

    def tchunk(dd, cc):
        return jnp.where(dd == 1, _rev_chunk(cc, 1, nch), cc)

    def plane_spec(plane):
        return pl.BlockSpec((None, n_groups, 8, S5_SCAN_LANES), lambda dd, lb, cc: (2 * dd + plane, tchunk(dd, cc), 0, lb))

    coef_spec = pl.BlockSpec((None, 8, 8, S5_SCAN_LANES), lambda dd, lb, cc: (dd, 0, 0, lb))
    x_spec = pl.BlockSpec((None, n_groups, 8, S5_SCAN_LANES), lambda dd, lb, cc: (dd, tchunk(dd, cc), 0, lb))
    x_shape = jax.ShapeDtypeStruct((2, t // 8, 8, S5_LANES), jnp.float32)
    x_re, x_im = pl.pallas_call(
        _s5_scan_body,
        grid=(2, S5_LANES // S5_SCAN_LANES, nch),
        in_specs=[plane_spec(0), plane_spec(1), coef_spec],
        out_specs=[x_spec, x_spec],
        out_shape=[x_shape, x_shape],
        scratch_shapes=[pltpu.VMEM((1, S5_SCAN_LANES), jnp.float32), pltpu.VMEM((1, S5_SCAN_LANES), jnp.float32)],
        compiler_params=pltpu.CompilerParams(dimension_semantics=("arbitrary", "arbitrary", "arbitrary"), vmem_limit_bytes=VMEM_LIMIT_BYTES),
        name="s5_scan",
    )(bu4, bu4, coef)

    xs_spec = pl.BlockSpec((2, S5_T_TILE, S5_LB), lambda j, i: (0, i, j))
    return pl.pallas_call(
        _s5_out_body,
        grid=(S5_NJ, nt),
        in_specs=[xs_spec, xs_spec,
                  pl.BlockSpec((S5_T_TILE, S5_CB), lambda j, i: (i, j)),
                  pl.BlockSpec((None, 4, S5_LB, S5_CB), lambda j, i: (j, 0, 0, 0)),
                  pl.BlockSpec((1, S5_CB), lambda j, i: (0, j))],
        out_specs=pl.BlockSpec((S5_T_TILE, S5_CB), lambda j, i: (i, j)),
        out_shape=jax.ShapeDtypeStruct((t, d), jnp.float32),
        compiler_params=pltpu.CompilerParams(dimension_semantics=("arbitrary", "arbitrary"), vmem_limit_bytes=VMEM_LIMIT_BYTES),
        name="s5_out",
    )(x_re.reshape(2, t, S5_LANES), x_im.reshape(2, t, S5_LANES), u, w3, d_skip.astype(jnp.float32).reshape(1, d))


def s5_mixer(u_ctx, u_lat, ctx_out, lam_re, lam_im, log_dt, b_re, b_im, c_re, c_im, d_skip):
    b_, tc, d = u_ctx.shape
    assert b_ == 1
    u = jnp.concatenate([u_ctx[0], u_lat[0]], axis=0).astype(jnp.float32)
    f = s5_pallas(u, tc, lam_re, lam_im, log_dt, b_re, b_im, c_re, c_im, d_skip)
    f_lat = f[None, tc:]
    f_ctx = f[None, :tc] if ctx_out else None
    return f_ctx, f_lat


def glu_out(f, w):
    gv = _mm(f, w)
    return gv[..., :D_MODEL] * jax.nn.sigmoid(gv[..., D_MODEL:])


def expert_choice_ffn(h, w_router, w_in, w_out):
    b_, n, d = h.shape
    cap = EC_CAPACITY * n // N_EXPERTS
    aff = jax.nn.softmax(jnp.einsum('bnd,de->bne', h, w_router).astype(jnp.float32), axis=-1)
    gate, idx = lax.top_k(jnp.swapaxes(aff, 1, 2), cap)
    xs = jax.vmap(lambda hb, ib: hb[ib])(h, idx)
    gu = _bmm(xs[0], w_in)
    act = jax.nn.silu(gu[..., :EXPERT_FF]) * gu[..., EXPERT_FF:]
    out = (_bmm(act, w_out) * gate[0][..., None].astype(h.dtype))[None]
    return jax.vmap(lambda ib, ob: jnp.zeros((n, d), ob.dtype).at[ib.reshape(-1)].add(ob.reshape(-1, d)))(idx, out)


def kernel(x, c, ctx, c_ctx, ada_w, ada_b, ln_g, ln_b, ev_w_in, ev_w_out, hy_conv, hy_w1, hy_b1, hy_w2, hy_b2, hy_w3, hy_b3, hy_w4, hy_freq, hy_bias, dn_conv, dn_a_log, dn_dt_bias, dn_norm_w, s5_lam_re, s5_lam_im, s5_log_dt, s5_b_re, s5_b_im, s5_c_re, s5_c_im, s5_d, od_w_glu, moe_router, moe_w_in, moe_w_out):
    d = D_MODEL
    h_lat, h_ctx = x, ctx
    s_lat = jax.nn.silu(c)
    s_ctx = jax.nn.silu(c_ctx)[None]
    for l in range(DEPTH):
        last = l == DEPTH - 1
        col = (l // 2) % 2 == 1
        i = l // 2
        n_ctx_mod = 2 if last else 6
        mod_lat = jnp.split((s_lat @ ada_w[l] + ada_b[l])[:, None], 6, axis=-1)
        mod_ctx = jnp.split((s_ctx @ ada_w[l][:, :n_ctx_mod * d] + ada_b[l][:n_ctx_mod * d])[:, None], n_ctx_mod, axis=-1)
        u_lat = h_lat * (1.0 + mod_lat[1]) + mod_lat[0]
        u_ctx = h_ctx * (1.0 + mod_ctx[1]) + mod_ctx[0]
        if col:
            u_lat = to_col_major(u_lat)
        if l % 2 == 0:
            f_ctx, f_lat = even_mixer(u_ctx, u_lat, not last, ev_w_in[i], hy_conv[i], hy_w1[i], hy_b1[i], hy_w2[i], hy_b2[i], hy_w3[i], hy_b3[i], hy_w4[i], hy_freq[i], hy_bias[i], dn_conv[i], dn_a_log[i], dn_dt_bias[i], dn_norm_w[i])
            w_proj = ev_w_out[i]
            proj = _mm
        else:
            f_ctx, f_lat = s5_mixer(u_ctx, u_lat, not last, s5_lam_re[i], s5_lam_im[i], s5_log_dt[i], s5_b_re[i], s5_b_im[i], s5_c_re[i], s5_c_im[i], s5_d[i])
            w_proj = od_w_glu[i]
            proj = glu_out
        if col:
            f_lat = from_col_major(f_lat)
        h_lat = post_norm(h_lat, mod_lat[2] * proj(f_lat, w_proj), ln_g[l, 0], ln_b[l, 0])
        m_lat = expert_choice_ffn(h_lat * (1.0 + mod_lat[4]) + mod_lat[3], moe_router[l], moe_w_in[l], moe_w_out[l])
        h_lat = post_norm(h_lat, mod_lat[5] * m_lat, ln_g[l, 1], ln_b[l, 1])
        if not last:
            h_ctx = post_norm(h_ctx, mod_ctx[2] * proj(f_ctx, w_proj), ln_g[l, 0], ln_b[l, 0])
            m_ctx = expert_choice_ffn(h_ctx * (1.0 + mod_ctx[4]) + mod_ctx[3], moe_router[l], moe_w_in[l], moe_w_out[l])
            h_ctx = post_norm(h_ctx, mod_ctx[5] * m_ctx, ln_g[l, 1], ln_b[l, 1])
    return h_lat
```

```python
import functools
import math

import jax
import jax.numpy as jnp
from jax import lax
from jax.experimental import pallas as pl
from jax.experimental.pallas import tpu as pltpu

D_MODEL = 4096
DEPTH = 4
CTX_LEN = 256
GRID_W = 64
HY_WIDTH = D_MODEL // 2
HY_ORDER = 2
HY_EMB = 33
HY_BANDS = (HY_EMB - 1) // 2
HY_TARGET = 1e-2
HY_MAX_DECAY = math.log(HY_TARGET) / 0.3
HY_MIN_DECAY = math.log(HY_TARGET) / 1.5
DN_HEADS = 16
DN_HEAD_DIM = 128
DN_WIDTH = DN_HEADS * DN_HEAD_DIM
DN_CHUNK = 64
S5_GROUP = 16
S5_GROUPS = D_MODEL // S5_GROUP
S5_STATE = 64
S5_BLOCK = 32
S5_MAX_RE = -1e-4
N_EXPERTS = 16
EXPERT_FF = 384
EC_CAPACITY = 2
DEEPNORM_ALPHA = (2 * DEPTH) ** 0.25
LN_EPS = 1e-5

VMEM_LIMIT_BYTES = 56 * 1024 * 1024


def _mm_body(a_ref, w_ref, o_ref, wb_ref):
    @pl.when(pl.program_id(2) == 0)
    def _():
        wb_ref[...] = w_ref[...].astype(jnp.bfloat16)

    o_ref[...] = jnp.dot(a_ref[...].astype(jnp.bfloat16), wb_ref[...], preferred_element_type=jnp.float32).astype(o_ref.dtype)


def _pick(n, cands):
    for c in cands:
        if n % c == 0:
            return c
    return n


def _bmm(a, w, out_dtype=jnp.float32):
    e, m, k = a.shape
    n = w.shape[-1]
    tm = _pick(m, (512, 256, 128, 64, 32, 16, 8))
    tn = _pick(n, (512, 384, 256, 128))
    return pl.pallas_call(
        _mm_body,
        grid=(e, n // tn, m // tm),
        in_specs=[
            pl.BlockSpec((None, tm, k), lambda ei, j, i: (ei, i, 0)),
            pl.BlockSpec((None, k, tn), lambda ei, j, i: (ei, 0, j)),
        ],
        out_specs=pl.BlockSpec((None, tm, tn), lambda ei, j, i: (ei, i, j)),
        out_shape=jax.ShapeDtypeStruct((e, m, n), out_dtype),
        scratch_shapes=[pltpu.VMEM((k, tn), jnp.bfloat16)],
        compiler_params=pltpu.CompilerParams(
            dimension_semantics=("arbitrary", "arbitrary", "arbitrary"),
            vmem_limit_bytes=VMEM_LIMIT_BYTES),
        name="bmm",
    )(a, w)


def _mm(a, w):
    lead = a.shape[:-1]
    out = _bmm(a.reshape(1, -1, a.shape[-1]).astype(jnp.bfloat16), w[None])
    return out.reshape(lead + (w.shape[-1],))


def post_norm(h, y, g, b):
    z = (DEEPNORM_ALPHA * h + y).astype(jnp.float32)
    mu = jnp.mean(z, axis=-1, keepdims=True)
    var = jnp.mean(jnp.square(z - mu), axis=-1, keepdims=True)
    return ((z - mu) * lax.rsqrt(var + LN_EPS) * g.astype(jnp.float32) + b.astype(jnp.float32)).astype(h.dtype)


def short_conv(u, w):
    k, ch = w.shape
    return lax.conv_general_dilated(u, w.astype(u.dtype)[:, None, :], window_strides=(1,), padding=[(k // 2, k // 2)], dimension_numbers=('NWC', 'WIO', 'NWC'), feature_group_count=ch)


def cat_streams(a_ctx, a_lat, rev):
    if rev:
        return jnp.concatenate([a_ctx[:, ::-1], a_lat[:, ::-1]], axis=1)
    return jnp.concatenate([a_ctx, a_lat], axis=1)


def to_col_major(h):
    b_, n, d = h.shape
    rows = n // GRID_W
    return h.reshape(b_, rows, GRID_W, d).transpose(0, 2, 1, 3).reshape(b_, n, d)


def from_col_major(h):
    b_, n, d = h.shape
    rows = n // GRID_W
    return h.reshape(b_, GRID_W, rows, d).transpose(0, 2, 1, 3).reshape(b_, n, d)


def hyena_filters(L, w1, b1, w2, b2, w3, b3, w4, freq):
    f32 = jnp.float32
    bands = jnp.linspace(1e-4, HY_BANDS - 1, HY_BANDS, dtype=f32)
    freq = freq.astype(f32)
    deltas = jnp.abs(jnp.linspace(HY_MIN_DECAY, HY_MAX_DECAY, HY_WIDTH, dtype=f32))
    w4d = w4.astype(f32).reshape(w4.shape[0], HY_ORDER, 2, HY_WIDTH)

    def taps(pos, direction):
        t = pos / max(L - 1, 1)
        ang = (2.0 * math.pi / L) * pos
        z = jnp.concatenate([t[:, None], jnp.cos(ang[:, None] * bands), -jnp.sin(ang[:, None] * bands)], axis=-1)
        h = jnp.sin(freq[0] * (z @ w1.astype(f32) + b1.astype(f32)))
        h = jnp.sin(freq[1] * (h @ w2.astype(f32) + b2.astype(f32)))
        h = jnp.sin(freq[2] * (h @ w3.astype(f32) + b3.astype(f32)))
        h = (h @ w4d[:, :, direction].reshape(w4.shape[0], HY_ORDER * HY_WIDTH)).reshape(pos.shape[0], HY_ORDER, HY_WIDTH)
        return h * jnp.exp(-t[:, None, None] * deltas)

    fwd = taps(jnp.arange(L, dtype=f32), 0)
    bwd_rev = taps(jnp.arange(L - 1, 0, -1, dtype=f32), 1)
    return jnp.concatenate([fwd, jnp.zeros_like(fwd[:1]), bwd_rev], axis=0)


def long_conv(u, filt, bias):
    L = u.shape[1]
    uf = jnp.fft.rfft(u.astype(jnp.float32), n=2 * L, axis=1)
    hf = jnp.fft.rfft(filt, axis=0)
    y = jnp.fft.irfft(uf * hf, n=2 * L, axis=1)[:, :L]
    return (y + u * bias).astype(u.dtype)


def hyena(hy, conv_w, filt_params, bias):
    L = hy.shape[1]
    v, x1, x2 = jnp.split(short_conv(hy, conv_w), 3, axis=-1)
    filt = hyena_filters(L, *filt_params)
    bias = bias.astype(jnp.float32)
    z = x1 * long_conv(v, filt[:, 0], bias[0])
    return x2 * long_conv(z, filt[:, 1], bias[1])


def l2norm(a):
    return a * lax.rsqrt(jnp.sum(a * a, axis=-1, keepdims=True) + 1e-6)


def gated_delta_chunked(q, k, v, g, beta):
    b_, t, h, dk = q.shape
    dv = v.shape[-1]
    n = t // DN_CHUNK

    def blocks(a):
        a = a.astype(jnp.float32).reshape((b_, n, DN_CHUNK, h) + a.shape[3:])
        return jnp.moveaxis(a, (1, 3), (0, 2))

    qc, kc, vc, bc = blocks(q), blocks(k), blocks(v), blocks(beta)
    gc = jnp.cumsum(blocks(g), axis=-1)
    causal = jnp.tril(jnp.ones((DN_CHUNK, DN_CHUNK), bool))
    strict = jnp.tril(jnp.ones((DN_CHUNK, DN_CHUNK), bool), -1)
    decay = jnp.exp(jnp.where(causal, gc[..., :, None] - gc[..., None, :], -jnp.inf))
    kb = kc * bc[..., None]
    a_mat = jnp.where(strict, jnp.einsum('nbhid,nbhjd->nbhij', kb, kc) * decay, 0.0)
    t_mat = a_mat + jnp.eye(DN_CHUNK, dtype=jnp.float32)
    u = lax.linalg.triangular_solve(t_mat, vc * bc[..., None], left_side=True, lower=True, unit_diagonal=True)
    w = lax.linalg.triangular_solve(t_mat, kb * jnp.exp(gc)[..., None], left_side=True, lower=True, unit_diagonal=True)
    qk = jnp.einsum('nbhid,nbhjd->nbhij', qc, kc) * decay

    def step(state, xs):
        q_i, k_i, u_i, w_i, g_i, qk_i = xs
        v_new = u_i - jnp.einsum('bhcd,bhde->bhce', w_i, state)
        o = jnp.einsum('bhcd,bhde->bhce', q_i * jnp.exp(g_i)[..., None], state) + jnp.einsum('bhij,bhje->bhie', qk_i, v_new)
        g_last = g_i[..., -1:]
        k_dec = k_i * jnp.exp(g_last - g_i)[..., None]
        state = state * jnp.exp(g_last)[..., None] + jnp.einsum('bhcd,bhce->bhde', k_dec, v_new)
        return state, o

    s0 = jnp.zeros((b_, h, dk, dv), jnp.float32)
    _, o = lax.scan(step, s0, (qc, kc, u, w, gc, qk))
    return jnp.moveaxis(o, (0, 2), (1, 3)).reshape(b_, t, h, dv)


def _split_bf16(a):
    hi = a.astype(jnp.bfloat16)
    lo = (a - hi.astype(jnp.float32)).astype(jnp.bfloat16)
    return hi, lo


def _dot3(a, b, dims):
    ah, al = _split_bf16(a)
    bh, bl = _split_bf16(b)
    dn = (dims, ((), ()))

    def dg(x, y):
        return lax.dot_general(x, y, dn, preferred_element_type=jnp.float32)

    return dg(ah, bh) + (dg(ah, bl) + dg(al, bh))


def _dot1(a, b, dims):
    return lax.dot_general(a.astype(jnp.bfloat16), b.astype(jnp.bfloat16), (dims, ((), ())), preferred_element_type=jnp.float32)


def _dot(a, b):
    return _dot3(a, b, ((1,), (0,)))


def _dot_nt(a, b):
    return _dot3(a, b, ((1,), (1,)))


def _dot_tn(a, b):
    return _dot3(a, b, ((0,), (0,)))


DN_HB = 4
DN_HG = DN_HEADS // DN_HB


def _delta_body(q_ref, k_ref, v_ref, gcol_ref, grow_ref, bcol_ref, o_ref, s_ref):
    rev = pl.program_id(0) >= DN_HG

    @pl.when(pl.program_id(1) == 0)
    def _():
        s_ref[...] = jnp.zeros_like(s_ref)

    c = DN_CHUNK
    row = lax.broadcasted_iota(jnp.int32, (c, c), 0)
    col = lax.broadcasted_iota(jnp.int32, (c, c), 1)
    lag = jnp.where(rev, col - row, row - col)
    incl = lag >= 0
    strict = lag > 0
    eye = (row == col).astype(jnp.float32)
    n_sq = c.bit_length() - 2
    mm = ((1,), (0,))
    row1 = lax.broadcasted_iota(jnp.int32, (c, 1), 0)
    last_row = row1 == jnp.where(rev, 0, c - 1)

    lanes = [pl.ds(hb * DN_HEAD_DIM, DN_HEAD_DIM) for hb in range(DN_HB)]
    loaded = [(q_ref[:, lanes[hb]], k_ref[:, lanes[hb]], v_ref[:, lanes[hb]], gcol_ref[hb], grow_ref[hb], bcol_ref[hb], s_ref[hb])
              for hb in range(DN_HB)]
    results = []
    for hb in range(DN_HB):
        q, k, v, gc_col, gc_row, beta, s = loaded[hb]
        g_tot = jnp.sum(jnp.where(last_row, gc_col, 0.0), axis=0, keepdims=True)
        decay = jnp.exp(jnp.where(incl, gc_col - gc_row, -1e30))

        kb = k * beta
        a_mat = jnp.where(strict, _dot_nt(kb, k) * decay, 0.0)
        x = eye - a_mat
        p = _dot1(a_mat, a_mat, mm)
        for it in range(n_sq):
            x = x + _dot1(x, p, mm)
            if it + 1 < n_sq:
                p = _dot1(p, p, mm)
        x = x + _dot(x, eye - _dot(eye + a_mat, x))
        u = _dot(x, v * beta)
        w = _dot(x, kb * jnp.exp(gc_col))
        qk = _dot1(q, k, ((1,), (1,))) * decay

        v_new = u - _dot(w, s)
        o = _dot1(q * jnp.exp(gc_col), s, ((1,), (0,))) + _dot1(qk, v_new, ((1,), (0,)))
        k_dec = k * jnp.exp(g_tot - gc_col)
        results.append((o, s * jnp.exp(g_tot) + _dot_tn(k_dec, v_new)))

    for hb in range(DN_HB):
        o_ref[:, lanes[hb]] = results[hb][0]
        s_ref[hb] = results[hb][1]


def _rev_chunk(cc, n_ctx_chunks, n_chunks):
    return jnp.where(cc < n_ctx_chunks, n_ctx_chunks - 1 - cc, n_chunks + n_ctx_chunks - 1 - cc)


def gated_delta_pallas(q, k, v, g, beta, n_ctx):
    t = q.shape[0]
    nch = t // DN_CHUNK
    ncc = n_ctx // DN_CHUNK
    dh = 2 * DN_HEADS
    gch = jnp.transpose(g, (0, 2, 1)).reshape(2, DN_HEADS, nch, DN_CHUNK)
    gc = jnp.stack([jnp.cumsum(gch[0], axis=-1), jnp.cumsum(gch[1, ..., ::-1], axis=-1)[..., ::-1]])
    g_col = gc.reshape(dh, t, 1)
    g_row = gc.reshape(dh, nch, 1, DN_CHUNK)
    b_col = jnp.transpose(beta, (0, 2, 1)).reshape(dh, t, 1)

    def chunk(i, cc):
        return jnp.where(i >= DN_HG, _rev_chunk(cc, ncc, nch), cc)

    qkv_spec = pl.BlockSpec((DN_CHUNK, DN_HB * DN_HEAD_DIM), lambda i, cc: (chunk(i, cc), i % DN_HG))
    col_spec = pl.BlockSpec((DN_HB, DN_CHUNK, 1), lambda i, cc: (i, chunk(i, cc), 0))
    return pl.pallas_call(
        _delta_body,
        grid=(2 * DN_HG, nch),
        in_specs=[qkv_spec, qkv_spec, qkv_spec, col_spec,
                  pl.BlockSpec((DN_HB, None, 1, DN_CHUNK), lambda i, cc: (i, chunk(i, cc), 0, 0)),
                  col_spec],
        out_specs=pl.BlockSpec((None, DN_CHUNK, DN_HB * DN_HEAD_DIM), lambda i, cc: (i // DN_HG, chunk(i, cc), i % DN_HG)),
        out_shape=jax.ShapeDtypeStruct((2, t, DN_WIDTH), jnp.float32),
        scratch_shapes=[pltpu.VMEM((DN_HB, DN_HEAD_DIM, DN_HEAD_DIM), jnp.float32)],
        compiler_params=pltpu.CompilerParams(dimension_semantics=("arbitrary", "arbitrary")),
        name="gated_delta",
    )(q, k, v, g_col, g_row, b_col)


def gated_head_norm(o, gate, w):
    b_, t_ = gate.shape[:2]
    o = o * lax.rsqrt(jnp.mean(o * o, axis=-1, keepdims=True) + 1e-6) * w.astype(jnp.float32)
    o = o * jax.nn.silu(gate.astype(jnp.float32)).reshape(b_, t_, DN_HEADS, DN_HEAD_DIM)
    return o.reshape(b_, t_, DN_WIDTH).astype(gate.dtype)


def deltanet(dn_ctx, dn_lat, ctx_out, conv_w, a_log, dt_bias, norm_w):
    a_rate = jnp.exp(a_log.astype(jnp.float32))
    dt_bias = dt_bias.astype(jnp.float32)

    def prep(dn):
        b_, t_, _ = dn.shape
        qkv = jax.nn.silu(short_conv(dn[..., :3 * DN_WIDTH], conv_w)).astype(jnp.float32)
        q, k, v = (a.reshape(b_, t_, DN_HEADS, DN_HEAD_DIM) for a in jnp.split(qkv, 3, axis=-1))
        ab = dn[..., 4 * DN_WIDTH:].astype(jnp.float32).reshape(b_, t_, 2, 2, DN_HEADS)
        beta = jax.nn.sigmoid(ab[:, :, 0])
        g = -a_rate * jax.nn.softplus(ab[:, :, 1] + dt_bias)
        return l2norm(q) * DN_HEAD_DIM ** -0.5, l2norm(k), v, beta, g

    pc, pl_ = prep(dn_ctx), prep(dn_lat)
    tc = dn_ctx.shape[1]
    b_ = dn_lat.shape[0]
    assert b_ == 1
    uni = [jnp.concatenate([a, b], axis=1)[0] for a, b in zip(pc, pl_)]
    t_all = uni[0].shape[0]
    o2 = gated_delta_pallas(uni[0].reshape(t_all, DN_WIDTH), uni[1].reshape(t_all, DN_WIDTH), uni[2].reshape(t_all, DN_WIDTH),
                            jnp.transpose(uni[4], (1, 0, 2)), jnp.transpose(uni[3], (1, 0, 2)), tc)
    o = (o2[0] + o2[1]).reshape(1, t_all, DN_HEADS, DN_HEAD_DIM)
    o_ctx, o_lat = o[:, :tc], o[:, tc:]
    y_lat = gated_head_norm(o_lat, dn_lat[..., 3 * DN_WIDTH:4 * DN_WIDTH], norm_w)
    y_ctx = gated_head_norm(o_ctx, dn_ctx[..., 3 * DN_WIDTH:4 * DN_WIDTH], norm_w) if ctx_out else None
    return y_ctx, y_lat


def even_mixer(u_ctx, u_lat, ctx_out, w_in, hy_conv, hy_w1, hy_b1, hy_w2, hy_b2, hy_w3, hy_b3, hy_w4, hy_freq, hy_bias, dn_conv, dn_a_log, dn_dt_bias, dn_norm_w):
    hy_cols = 3 * HY_WIDTH
    n_main = 7 * HY_WIDTH
    filt = (hy_w1, hy_b1, hy_w2, hy_b2, hy_w3, hy_b3, hy_w4, hy_freq)
    w_tail = w_in[:, n_main:]

    def proj_in(u):
        lead = u.shape[:-1]
        a = u.reshape(1, -1, u.shape[-1]).astype(jnp.bfloat16)
        m = a.shape[1]
        tm = _pick(m, (512, 256))
        tn = 512
        main = pl.pallas_call(
            _mm_body,
            grid=(1, n_main // tn, m // tm),
            in_specs=[
                pl.BlockSpec((None, tm, D_MODEL), lambda ei, j, i: (ei, i, 0)),
                pl.BlockSpec((None, D_MODEL, tn), lambda ei, j, i: (ei, 0, j)),
            ],
            out_specs=pl.BlockSpec((None, tm, tn), lambda ei, j, i: (ei, i, j)),
            out_shape=jax.ShapeDtypeStruct((1, m, n_main), jnp.float32),
            scratch_shapes=[pltpu.VMEM((D_MODEL, tn), jnp.bfloat16)],
            compiler_params=pltpu.CompilerParams(
                dimension_semantics=("arbitrary", "arbitrary", "arbitrary"),
                vmem_limit_bytes=VMEM_LIMIT_BYTES),
            name="proj_in",
        )(a, w_in[None])
        tail = _bmm(a, w_tail[None])
        return jnp.concatenate([main, tail], axis=-1).reshape(lead + (w_in.shape[-1],))

    z_lat = proj_in(u_lat)
    z_ctx = proj_in(u_ctx)
    dn_ctx = z_ctx[..., hy_cols:]
    y_dn_ctx, y_dn_lat = deltanet(dn_ctx, z_lat[..., hy_cols:], ctx_out, dn_conv, dn_a_log, dn_dt_bias, dn_norm_w)
    y_lat = jnp.concatenate([hyena(z_lat[..., :hy_cols], hy_conv, filt, hy_bias), y_dn_lat], axis=-1)
    y_ctx = jnp.concatenate([hyena(z_ctx[..., :hy_cols], hy_conv, filt, hy_bias), y_dn_ctx], axis=-1) if ctx_out else None
    return y_ctx, y_lat


S5_GB = 8
S5_CB = S5_GB * S5_GROUP
S5_LB = S5_GB * S5_STATE
S5_NJ = S5_GROUPS // S5_GB
S5_LANES = S5_GROUPS * S5_STATE
S5_T_TILE = 1056
S5_SCAN_CHUNK = 256
S5_SCAN_LANES = 1024


def _s5_params(lam_re, lam_im, log_dt, b_re, b_im, c_re, c_im):
    f32 = jnp.float32
    lam = lax.complex(jnp.minimum(lam_re.astype(f32), S5_MAX_RE), lam_im.astype(f32))
    lam_bar = jnp.exp(lam * jnp.exp(log_dt.astype(f32))[..., None])
    b_bar = ((lam_bar - 1.0) / lam)[..., None] * lax.complex(b_re.astype(f32), b_im.astype(f32))
    eye = jnp.eye(S5_GB, dtype=f32)

    def w_in(part):
        part = part.reshape(2, S5_NJ, S5_GB, S5_STATE, S5_GROUP)
        return jnp.einsum('gh,djhpc->djgchp', eye, part).reshape(2, S5_NJ, S5_CB, S5_LB)

    def w_out(part):
        part = part.reshape(2, S5_NJ, S5_GB, S5_GROUP, S5_STATE)
        return jnp.einsum('gh,djhcp->djhpgc', eye, part).reshape(2, S5_NJ, S5_LB, S5_CB)

    wi_re, wi_im = w_in(jnp.real(b_bar)), w_in(jnp.imag(b_bar))
    w1 = jnp.stack([wi_re[0], wi_im[0], wi_re[1], wi_im[1]], axis=2).reshape(S5_NJ, S5_CB, 4 * S5_LB)
    wo_re, wo_im = w_out(c_re.astype(f32)), w_out(-c_im.astype(f32))
    w3 = jnp.stack([wo_re[0], wo_im[0], wo_re[1], wo_im[1]], axis=1)
    a = lam_bar.reshape(2, 1, S5_LANES)
    pows = [a]
    for _ in range(7):
        pows.append(pows[-1] * a)
    pows = jnp.concatenate(pows, axis=1)
    r = jnp.arange(8)[None, :, None]
    done = jnp.stack([r, 7 - r])
    done = done.reshape(2, 8, 1)
    tabs = []
    for sh in (1, 2, 4):
        tabs.append(jnp.where(done >= sh, pows[:, sh - 1:sh], 0.0))
    tabs.append(jnp.stack([pows[0], pows[1, ::-1]]))
    coef = jnp.stack([part for tab in tabs for part in (jnp.real(tab), jnp.imag(tab))], axis=1)
    return w1, w3, coef


def _s5_bu_body(u_ref, w_ref, o_ref):
    y = jnp.dot(u_ref[...].astype(jnp.bfloat16), w_ref[...].astype(jnp.bfloat16), preferred_element_type=jnp.float32)
    for q in range(4):
        o_ref[q] = y[:, q * S5_LB:(q + 1) * S5_LB]


def _s5_scan_body(bre_ref, bim_ref, coef_ref, xre_ref, xim_ref, cre_ref, cim_ref):

    @pl.when(pl.program_id(2) == 0)
    def _():
        cre_ref[...] = jnp.zeros_like(cre_ref)
        cim_ref[...] = jnp.zeros_like(cim_ref)

    n_groups = S5_SCAN_CHUNK // 8

    def run(rev):
        br = bre_ref[...]
        bi = bim_ref[...]
        for n, sh in enumerate((1, 2, 4)):
            are = coef_ref[2 * n][None]
            aim = coef_ref[2 * n + 1][None]
            shift = 8 - sh if rev else sh
            rr = pltpu.roll(br, shift, 1)
            ri = pltpu.roll(bi, shift, 1)
            br, bi = br + are * rr - aim * ri, bi + are * ri + aim * rr
        pre = coef_ref[6]
        pim = coef_ref[7]
        cr = cre_ref[...]
        ci = cim_ref[...]
        last = 0 if rev else 7
        for g in (range(n_groups - 1, -1, -1) if rev else range(n_groups)):
            xr = br[g] + pre * cr - pim * ci
            xi = bi[g] + pre * ci + pim * cr
            xre_ref[g] = xr
            xim_ref[g] = xi
            cr = xr[last:last + 1]
            ci = xi[last:last + 1]
        cre_ref[...] = cr
        cim_ref[...] = ci

    @pl.when(pl.program_id(0) == 0)
    def _():
        run(False)

    @pl.when(pl.program_id(0) == 1)
    def _():
        run(True)


def _s5_out_body(xre_ref, xim_ref, u_ref, w_ref, d_ref, o_ref):
    acc = None
    for d in range(2):
        for x_ref, q in ((xre_ref, 2 * d), (xim_ref, 2 * d + 1)):
            part = jnp.dot(x_ref[d].astype(jnp.bfloat16), w_ref[q].astype(jnp.bfloat16), preferred_element_type=jnp.float32)
            acc = part if acc is None else acc + part
    o_ref[...] = jax.nn.gelu(acc + d_ref[...] * u_ref[...])


def s5_pallas(u, n_ctx, lam_re, lam_im, log_dt, b_re, b_im, c_re, c_im, d_skip):
    t, d = u.shape
    w1, w3, coef = _s5_params(lam_re, lam_im, log_dt, b_re, b_im, c_re, c_im)
    nt = t // S5_T_TILE
    bu = pl.pallas_call(
        _s5_bu_body,
        grid=(S5_NJ, nt),
        in_specs=[pl.BlockSpec((S5_T_TILE, S5_CB), lambda j, i: (i, j)),
                  pl.BlockSpec((None, S5_CB, 4 * S5_LB), lambda j, i: (j, 0, 0))],
        out_specs=pl.BlockSpec((4, S5_T_TILE, S5_LB), lambda j, i: (0, i, j)),
        out_shape=jax.ShapeDtypeStruct((4, t, S5_LANES), jnp.float32),
        compiler_params=pltpu.CompilerParams(dimension_semantics=("arbitrary", "arbitrary"), vmem_limit_bytes=VMEM_LIMIT_BYTES),
        name="s5_bu",
    )(u, w1)

    assert n_ctx == S5_SCAN_CHUNK
    nch = t // S5_SCAN_CHUNK
    n_groups = S5_SCAN_CHUNK // 8
    bu4 = bu.reshape(4, t // 8, 8, S5_LANES)

    def tchunk(dd, cc):
        return jnp.where(dd == 1, _rev_chunk(cc, 1, nch), cc)

    def plane_spec(plane):
        return pl.BlockSpec((None, n_groups, 8, S5_SCAN_LANES), lambda dd, lb, cc: (2 * dd + plane, tchunk(dd, cc), 0, lb))

    coef_spec = pl.BlockSpec((None, 8, 8, S5_SCAN_LANES), lambda dd, lb, cc: (dd, 0, 0, lb))
    x_spec = pl.BlockSpec((None, n_groups, 8, S5_SCAN_LANES), lambda dd, lb, cc: (dd, tchunk(dd, cc), 0, lb))
    x_shape = jax.ShapeDtypeStruct((2, t // 8, 8, S5_LANES), jnp.float32)
    x_re, x_im = pl.pallas_call(
        _s5_scan_body,
        grid=(2, S5_LANES // S5_SCAN_LANES, nch),
        in_specs=[plane_spec(0), plane_spec(1), coef_spec],
        out_specs=[x_spec, x_spec],
        out_shape=[x_shape, x_shape],
        scratch_shapes=[pltpu.VMEM((1, S5_SCAN_LANES), jnp.float32), pltpu.VMEM((1, S5_SCAN_LANES), jnp.float32)],
        compiler_params=pltpu.CompilerParams(dimension_semantics=("arbitrary", "arbitrary", "arbitrary"), vmem_limit_bytes=VMEM_LIMIT_BYTES),
        name="s5_scan",
    )(bu4, bu4, coef)

    xs_spec = pl.BlockSpec((2, S5_T_TILE, S5_LB), lambda j, i: (0, i, j))
    return pl.pallas_call(
        _s5_out_body,
        grid=(S5_NJ, nt),
        in_specs=[xs_spec, xs_spec,
                  pl.BlockSpec((S5_T_TILE, S5_CB), lambda j, i: (i, j)),
                  pl.BlockSpec((None, 4, S5_LB, S5_CB), lambda j, i: (j, 0, 0, 0)),
                  pl.BlockSpec((1, S5_CB), lambda j, i: (0, j))],
        out_specs=pl.BlockSpec((S5_T_TILE, S5_CB), lambda j, i: (i, j)),
        out_shape=jax.ShapeDtypeStruct((t, d), jnp.float32),
        compiler_params=pltpu.CompilerParams(dimension_semantics=("arbitrary", "arbitrary"), vmem_limit_bytes=VMEM_LIMIT_BYTES),
        name="s5_out",
    )(x_re.reshape(2, t, S5_LANES), x_im.reshape(2, t, S5_LANES), u, w3, d_skip.astype(jnp.float32).reshape(1, d))


def s5_mixer(u_ctx, u_lat, ctx_out, lam_re, lam_im, log_dt, b_re, b_im, c_re, c_im, d_skip):
    b_, tc, d = u_ctx.shape
    assert b_ == 1
    u = jnp.concatenate([u_ctx[0], u_lat[0]], axis=0).astype(jnp.float32)
    f = s5_pallas(u, tc, lam_re, lam_im, log_dt, b_re, b_im, c_re, c_im, d_skip)
    f_lat = f[None, tc:]
    f_ctx = f[None, :tc] if ctx_out else None
    return f_ctx, f_lat


def glu_out(f, w):
    gv = _mm(f, w)
    return gv[..., :D_MODEL] * jax.nn.sigmoid(gv[..., D_MODEL:])


def expert_choice_ffn(h, w_router, w_in, w_out):
    b_, n, d = h.shape
    cap = EC_CAPACITY * n // N_EXPERTS
    aff = jax.nn.softmax(jnp.einsum('bnd,de->bne', h, w_router).astype(jnp.float32), axis=-1)
    gate, idx = lax.top_k(jnp.swapaxes(aff, 1, 2), cap)
    xs = jax.vmap(lambda hb, ib: hb[ib])(h, idx)
    gu = _bmm(xs[0], w_in)
    act = jax.nn.silu(gu[..., :EXPERT_FF]) * gu[..., EXPERT_FF:]
    out = (_bmm(act, w_out) * gate[0][..., None].astype(h.dtype))[None]
    return jax.vmap(lambda ib, ob: jnp.zeros((n, d), ob.dtype).at[ib.reshape(-1)].add(ob.reshape(-1, d)))(idx, out)


def kernel(x, c, ctx, c_ctx, ada_w, ada_b, ln_g, ln_b, ev_w_in, ev_w_out, hy_conv, hy_w1, hy_b1, hy_w2, hy_b2, hy_w3, hy_b3, hy_w4, hy_freq, hy_bias, dn_conv, dn_a_log, dn_dt_bias, dn_norm_w, s5_lam_re, s5_lam_im, s5_log_dt, s5_b_re, s5_b_im, s5_c_re, s5_c_im, s5_d, od_w_glu, moe_router, moe_w_in, moe_w_out):
    d = D_MODEL
    h_lat, h_ctx = x, ctx
    s_lat = jax.nn.silu(c)
    s_ctx = jax.nn.silu(c_ctx)[None]
    for l in range(DEPTH):
        last = l == DEPTH - 1
        col = (l // 2) % 2 == 1
        i = l // 2
        n_ctx_mod = 2 if last else 6
        mod_lat = jnp.split((s_lat @ ada_w[l] + ada_b[l])[:, None], 6, axis=-1)
        mod_ctx = jnp.split((s_ctx @ ada_w[l][:, :n_ctx_mod * d] + ada_b[l][:n_ctx_mod * d])[:, None], n_ctx_mod, axis=-1)
        u_lat = h_lat * (1.0 + mod_lat[1]) + mod_lat[0]
        u_ctx = h_ctx * (1.0 + mod_ctx[1]) + mod_ctx[0]
        if col:
            u_lat = to_col_major(u_lat)
        if l % 2 == 0:
            f_ctx, f_lat = even_mixer(u_ctx, u_lat, not last, ev_w_in[i], hy_conv[i], hy_w1[i], hy_b1[i], hy_w2[i], hy_b2[i], hy_w3[i], hy_b3[i], hy_w4[i], hy_freq[i], hy_bias[i], dn_conv[i], dn_a_log[i], dn_dt_bias[i], dn_norm_w[i])
            w_proj = ev_w_out[i]
            proj = _mm
        else:
            f_ctx, f_lat = s5_mixer(u_ctx, u_lat, not last, s5_lam_re[i], s5_lam_im[i], s5_log_dt[i], s5_b_re[i], s5_b_im[i], s5_c_re[i], s5_c_im[i], s5_d[i])
            w_proj = od_w_glu[i]
            proj = glu_out
        if col:
            f_lat = from_col_major(f_lat)
        h_lat = post_norm(h_lat, mod_lat[2] * proj(f_lat, w_proj), ln_g[l, 0], ln_b[l, 0])
        m_lat = expert_choice_ffn(h_lat * (1.0 + mod_lat[4]) + mod_lat[3], moe_router[l], moe_w_in[l], moe_w_out[l])
        h_lat = post_norm(h_lat, mod_lat[5] * m_lat, ln_g[l, 1], ln_b[l, 1])
        if not last:
            h_ctx = post_norm(h_ctx, mod_ctx[2] * proj(f_ctx, w_proj), ln_g[l, 0], ln_b[l, 0])
            m_ctx = expert_choice_ffn(h_ctx * (1.0 + mod_ctx[4]) + mod_ctx[3], moe_router[l], moe_w_in[l], moe_w_out[l])
            h_ctx = post_norm(h_ctx, mod_ctx[5] * m_ctx, ln_g[l, 1], ln_b[l, 1])
    return h_lat
```

```python
import functools
import math

import jax
import jax.numpy as jnp
from jax import lax
from jax.experimental import pallas as pl
from jax.experimental.pallas import tpu as pltpu

D_MODEL = 4096
DEPTH = 4
CTX_LEN = 256
GRID_W = 64
HY_WIDTH = D_MODEL // 2
HY_ORDER = 2
HY_EMB = 33
HY_BANDS = (HY_EMB - 1) // 2
HY_TARGET = 1e-2
HY_MAX_DECAY = math.log(HY_TARGET) / 0.3
HY_MIN_DECAY = math.log(HY_TARGET) / 1.5
DN_HEADS = 16
DN_HEAD_DIM = 128
DN_WIDTH = DN_HEADS * DN_HEAD_DIM
DN_CHUNK = 64
S5_GROUP = 16
S5_GROUPS = D_MODEL // S5_GROUP
S5_STATE = 64
S5_BLOCK = 32
S5_MAX_RE = -1e-4
N_EXPERTS = 16
EXPERT_FF = 384
EC_CAPACITY = 2
DEEPNORM_ALPHA = (2 * DEPTH) ** 0.25
LN_EPS = 1e-5

VMEM_LIMIT_BYTES = 56 * 1024 * 1024


def _mm_body(a_ref, w_ref, o_ref, wb_ref):
    @pl.when(pl.program_id(2) == 0)
    def _():
        wb_ref[...] = w_ref[...].astype(jnp.bfloat16)

    o_ref[...] = jnp.dot(a_ref[...].astype(jnp.bfloat16), wb_ref[...], preferred_element_type=jnp.float32).astype(o_ref.dtype)


def _pick(n, cands):
    for c in cands:
        if n % c == 0:
            return c
    return n


def _bmm(a, w, out_dtype=jnp.float32):
    e, m, k = a.shape
    n = w.shape[-1]
    tm = _pick(m, (512, 256, 128, 64, 32, 16, 8))
    tn = _pick(n, (512, 384, 256, 128))
    return pl.pallas_call(
        _mm_body,
        grid=(e, n // tn, m // tm),
        in_specs=[
            pl.BlockSpec((None, tm, k), lambda ei, j, i: (ei, i, 0)),
            pl.BlockSpec((None, k, tn), lambda ei, j, i: (ei, 0, j)),
        ],
        out_specs=pl.BlockSpec((None, tm, tn), lambda ei, j, i: (ei, i, j)),
        out_shape=jax.ShapeDtypeStruct((e, m, n), out_dtype),
        scratch_shapes=[pltpu.VMEM((k, tn), jnp.bfloat16)],
        compiler_params=pltpu.CompilerParams(
            dimension_semantics=("arbitrary", "arbitrary", "arbitrary"),
            vmem_limit_bytes=VMEM_LIMIT_BYTES),
        name="bmm",
    )(a, w)


def _mm(a, w):
    lead = a.shape[:-1]
    out = _bmm(a.reshape(1, -1, a.shape[-1]).astype(jnp.bfloat16), w[None])
    return out.reshape(lead + (w.shape[-1],))


def post_norm(h, y, g, b):
    z = (DEEPNORM_ALPHA * h + y).astype(jnp.float32)
    mu = jnp.mean(z, axis=-1, keepdims=True)
    var = jnp.mean(jnp.square(z - mu), axis=-1, keepdims=True)
    return ((z - mu) * lax.rsqrt(var + LN_EPS) * g.astype(jnp.float32) + b.astype(jnp.float32)).astype(h.dtype)


def short_conv(u, w):
    k, ch = w.shape
    return lax.conv_general_dilated(u, w.astype(u.dtype)[:, None, :], window_strides=(1,), padding=[(k // 2, k // 2)], dimension_numbers=('NWC', 'WIO', 'NWC'), feature_group_count=ch)


def cat_streams(a_ctx, a_lat, rev):
    if rev:
        return jnp.concatenate([a_ctx[:, ::-1], a_lat[:, ::-1]], axis=1)
    return jnp.concatenate([a_ctx, a_lat], axis=1)


def to_col_major(h):
    b_, n, d = h.shape
    rows = n // GRID_W
    return h.reshape(b_, rows, GRID_W, d).transpose(0, 2, 1, 3).reshape(b_, n, d)


def from_col_major(h):
    b_, n, d = h.shape
    rows = n // GRID_W
    return h.reshape(b_, GRID_W, rows, d).transpose(0, 2, 1, 3).reshape(b_, n, d)


def hyena_filters(L, w1, b1, w2, b2, w3, b3, w4, freq):
    f32 = jnp.float32
    bands = jnp.linspace(1e-4, HY_BANDS - 1, HY_BANDS, dtype=f32)
    freq = freq.astype(f32)
    deltas = jnp.abs(jnp.linspace(HY_MIN_DECAY, HY_MAX_DECAY, HY_WIDTH, dtype=f32))
    w4d = w4.astype(f32).reshape(w4.shape[0], HY_ORDER, 2, HY_WIDTH)

    def taps(pos, direction):
        t = pos / max(L - 1, 1)
        ang = (2.0 * math.pi / L) * pos
        z = jnp.concatenate([t[:, None], jnp.cos(ang[:, None] * bands), -jnp.sin(ang[:, None] * bands)], axis=-1)
        h = jnp.sin(freq[0] * (z @ w1.astype(f32) + b1.astype(f32)))
        h = jnp.sin(freq[1] * (h @ w2.astype(f32) + b2.astype(f32)))
        h = jnp.sin(freq[2] * (h @ w3.astype(f32) + b3.astype(f32)))
        h = (h @ w4d[:, :, direction].reshape(w4.shape[0], HY_ORDER * HY_WIDTH)).reshape(pos.shape[0], HY_ORDER, HY_WIDTH)
        return h * jnp.exp(-t[:, None, None] * deltas)

    fwd = taps(jnp.arange(L, dtype=f32), 0)
    bwd_rev = taps(jnp.arange(L - 1, 0, -1, dtype=f32), 1)
    return jnp.concatenate([fwd, jnp.zeros_like(fwd[:1]), bwd_rev], axis=0)


def long_conv(u, filt, bias):
    L = u.shape[1]
    uf = jnp.fft.rfft(u.astype(jnp.float32), n=2 * L, axis=1)
    hf = jnp.fft.rfft(filt, axis=0)
    y = jnp.fft.irfft(uf * hf, n=2 * L, axis=1)[:, :L]
    return (y + u * bias).astype(u.dtype)


def hyena(hy, conv_w, filt_params, bias):
    L = hy.shape[1]
    v, x1, x2 = jnp.split(short_conv(hy, conv_w), 3, axis=-1)
    filt = hyena_filters(L, *filt_params)
    bias = bias.astype(jnp.float32)
    z = x1 * long_conv(v, filt[:, 0], bias[0])
    return x2 * long_conv(z, filt[:, 1], bias[1])


def l2norm(a):
    return a * lax.rsqrt(jnp.sum(a * a, axis=-1, keepdims=True) + 1e-6)


def gated_delta_chunked(q, k, v, g, beta):
    b_, t, h, dk = q.shape
    dv = v.shape[-1]
    n = t // DN_CHUNK

    def blocks(a):
        a = a.astype(jnp.float32).reshape((b_, n, DN_CHUNK, h) + a.shape[3:])
        return jnp.moveaxis(a, (1, 3), (0, 2))

    qc, kc, vc, bc = blocks(q), blocks(k), blocks(v), blocks(beta)
    gc = jnp.cumsum(blocks(g), axis=-1)
    causal = jnp.tril(jnp.ones((DN_CHUNK, DN_CHUNK), bool))
    strict = jnp.tril(jnp.ones((DN_CHUNK, DN_CHUNK), bool), -1)
    decay = jnp.exp(jnp.where(causal, gc[..., :, None] - gc[..., None, :], -jnp.inf))
    kb = kc * bc[..., None]
    a_mat = jnp.where(strict, jnp.einsum('nbhid,nbhjd->nbhij', kb, kc) * decay, 0.0)
    t_mat = a_mat + jnp.eye(DN_CHUNK, dtype=jnp.float32)
    u = lax.linalg.triangular_solve(t_mat, vc * bc[..., None], left_side=True, lower=True, unit_diagonal=True)
    w = lax.linalg.triangular_solve(t_mat, kb * jnp.exp(gc)[..., None], left_side=True, lower=True, unit_diagonal=True)
    qk = jnp.einsum('nbhid,nbhjd->nbhij', qc, kc) * decay

    def step(state, xs):
        q_i, k_i, u_i, w_i, g_i, qk_i = xs
        v_new = u_i - jnp.einsum('bhcd,bhde->bhce', w_i, state)
        o = jnp.einsum('bhcd,bhde->bhce', q_i * jnp.exp(g_i)[..., None], state) + jnp.einsum('bhij,bhje->bhie', qk_i, v_new)
        g_last = g_i[..., -1:]
        k_dec = k_i * jnp.exp(g_last - g_i)[..., None]
        state = state * jnp.exp(g_last)[..., None] + jnp.einsum('bhcd,bhce->bhde', k_dec, v_new)
        return state, o

    s0 = jnp.zeros((b_, h, dk, dv), jnp.float32)
    _, o = lax.scan(step, s0, (qc, kc, u, w, gc, qk))
    return jnp.moveaxis(o, (0, 2), (1, 3)).reshape(b_, t, h, dv)


def _split_bf16(a):
    hi = a.astype(jnp.bfloat16)
    lo = (a - hi.astype(jnp.float32)).astype(jnp.bfloat16)
    return hi, lo


def _dot3(a, b, dims):
    ah, al = _split_bf16(a)
    bh, bl = _split_bf16(b)
    dn = (dims, ((), ()))

    def dg(x, y):
        return lax.dot_general(x, y, dn, preferred_element_type=jnp.float32)

    return dg(ah, bh) + (dg(ah, bl) + dg(al, bh))


def _dot1(a, b, dims):
    return lax.dot_general(a.astype(jnp.bfloat16), b.astype(jnp.bfloat16), (dims, ((), ())), preferred_element_type=jnp.float32)


def _dot(a, b):
    return _dot3(a, b, ((1,), (0,)))


def _dot_nt(a, b):
    return _dot3(a, b, ((1,), (1,)))


def _dot_tn(a, b):
    return _dot3(a, b, ((0,), (0,)))


DN_HB = 16
DN_HG = DN_HEADS // DN_HB


def _delta_body(q_ref, k_ref, v_ref, gcol_ref, grow_ref, bcol_ref, o_ref, s_ref):
    rev = pl.program_id(0) >= DN_HG

    @pl.when(pl.program_id(1) == 0)
    def _():
        s_ref[...] = jnp.zeros_like(s_ref)

    c = DN_CHUNK
    row = lax.broadcasted_iota(jnp.int32, (c, c), 0)
    col = lax.broadcasted_iota(jnp.int32, (c, c), 1)
    lag = jnp.where(rev, col - row, row - col)
    incl = lag >= 0
    strict = lag > 0
    eye = (row == col).astype(jnp.float32)
    n_sq = c.bit_length() - 2
    mm = ((1,), (0,))
    row1 = lax.broadcasted_iota(jnp.int32, (c, 1), 0)
    last_row = row1 == jnp.where(rev, 0, c - 1)

    lanes = [pl.ds(hb * DN_HEAD_DIM, DN_HEAD_DIM) for hb in range(DN_HB)]
    q = jnp.stack([q_ref[:, ln] for ln in lanes])
    k = jnp.stack([k_ref[:, ln] for ln in lanes])
    v = jnp.stack([v_ref[:, ln] for ln in lanes])
    gc_col = gcol_ref[...]
    gc_row = grow_ref[...]
    beta = bcol_ref[...]
    s = s_ref[...]
    nn, nt = (2, 1), (2, 2)

    def b1(a, b, dims):
        return lax.dot_general(a.astype(jnp.bfloat16), b.astype(jnp.bfloat16), (((dims[0],), (dims[1],)), ((0,), (0,))),
                               preferred_element_type=jnp.float32)

    def b3(a, b, dims):
        ah, al = _split_bf16(a)
        bh, bl = _split_bf16(b)
        dn = (((dims[0],), (dims[1],)), ((0,), (0,)))

        def dg(x, y):
            return lax.dot_general(x, y, dn, preferred_element_type=jnp.float32)

        return dg(ah, bh) + (dg(ah, bl) + dg(al, bh))

    g_tot = jnp.sum(jnp.where(last_row, gc_col, 0.0), axis=1, keepdims=True)
    decay = jnp.exp(jnp.where(incl, gc_col - gc_row, -1e30))

    kb = k * beta
    a_mat = jnp.where(strict, b3(kb, k, nt) * decay, 0.0)
    x = eye - a_mat
    p = b1(a_mat, a_mat, nn)
    for it in range(n_sq):
        x = x + b1(x, p, nn)
        if it + 1 < n_sq:
            p = b1(p, p, nn)
    x = x + b3(x, eye - b3(eye + a_mat, x, nn), nn)
    u = b3(x, v * beta, nn)
    w = b3(x, kb * jnp.exp(gc_col), nn)
    qk = b1(q, k, nt) * decay

    v_new = u - b3(w, s, nn)
    o = b1(q * jnp.exp(gc_col), s, nn) + b1(qk, v_new, nn)
    k_dec_t = jnp.swapaxes(k * jnp.exp(g_tot - gc_col), 1, 2)
    s_ref[...] = s * jnp.exp(g_tot) + b3(k_dec_t, v_new, nn)
    for hb in range(DN_HB):
        o_ref[:, lanes[hb]] = o[hb]


def _rev_chunk(cc, n_ctx_chunks, n_chunks):
    return jnp.where(cc < n_ctx_chunks, n_ctx_chunks - 1 - cc, n_chunks + n_ctx_chunks - 1 - cc)


def gated_delta_pallas(q, k, v, g, beta, n_ctx):
    t = q.shape[0]
    nch = t // DN_CHUNK
    ncc = n_ctx // DN_CHUNK
    dh = 2 * DN_HEADS
    gch = jnp.transpose(g, (0, 2, 1)).reshape(2, DN_HEADS, nch, DN_CHUNK)
    gc = jnp.stack([jnp.cumsum(gch[0], axis=-1), jnp.cumsum(gch[1, ..., ::-1], axis=-1)[..., ::-1]])
    g_col = gc.reshape(dh, t, 1)
    g_row = gc.reshape(dh, nch, 1, DN_CHUNK)
    b_col = jnp.transpose(beta, (0, 2, 1)).reshape(dh, t, 1)

    def chunk(i, cc):
        return jnp.where(i >= DN_HG, _rev_chunk(cc, ncc, nch), cc)

    qkv_spec = pl.BlockSpec((DN_CHUNK, DN_HB * DN_HEAD_DIM), lambda i, cc: (chunk(i, cc), i % DN_HG))
    col_spec = pl.BlockSpec((DN_HB, DN_CHUNK, 1), lambda i, cc: (i, chunk(i, cc), 0))
    return pl.pallas_call(
        _delta_body,
        grid=(2 * DN_HG, nch),
        in_specs=[qkv_spec, qkv_spec, qkv_spec, col_spec,
                  pl.BlockSpec((DN_HB, None, 1, DN_CHUNK), lambda i, cc: (i, chunk(i, cc), 0, 0)),
                  col_spec],
        out_specs=pl.BlockSpec((None, DN_CHUNK, DN_HB * DN_HEAD_DIM), lambda i, cc: (i // DN_HG, chunk(i, cc), i % DN_HG)),
        out_shape=jax.ShapeDtypeStruct((2, t, DN_WIDTH), jnp.float32),
        scratch_shapes=[pltpu.VMEM((DN_HB, DN_HEAD_DIM, DN_HEAD_DIM), jnp.float32)],
        compiler_params=pltpu.CompilerParams(dimension_semantics=("arbitrary", "arbitrary")),
        name="gated_delta",
    )(q, k, v, g_col, g_row, b_col)


def gated_head_norm(o, gate, w):
    b_, t_ = gate.shape[:2]
    o = o * lax.rsqrt(jnp.mean(o * o, axis=-1, keepdims=True) + 1e-6) * w.astype(jnp.float32)
    o = o * jax.nn.silu(gate.astype(jnp.float32)).reshape(b_, t_, DN_HEADS, DN_HEAD_DIM)
    return o.reshape(b_, t_, DN_WIDTH).astype(gate.dtype)


def deltanet(dn_ctx, dn_lat, ctx_out, conv_w, a_log, dt_bias, norm_w):
    a_rate = jnp.exp(a_log.astype(jnp.float32))
    dt_bias = dt_bias.astype(jnp.float32)

    def prep(dn):
        b_, t_, _ = dn.shape
        qkv = jax.nn.silu(short_conv(dn[..., :3 * DN_WIDTH], conv_w)).astype(jnp.float32)
        q, k, v = (a.reshape(b_, t_, DN_HEADS, DN_HEAD_DIM) for a in jnp.split(qkv, 3, axis=-1))
        ab = dn[..., 4 * DN_WIDTH:].astype(jnp.float32).reshape(b_, t_, 2, 2, DN_HEADS)
        beta = jax.nn.sigmoid(ab[:, :, 0])
        g = -a_rate * jax.nn.softplus(ab[:, :, 1] + dt_bias)
        return l2norm(q) * DN_HEAD_DIM ** -0.5, l2norm(k), v, beta, g

    pc, pl_ = prep(dn_ctx), prep(dn_lat)
    tc = dn_ctx.shape[1]
    b_ = dn_lat.shape[0]
    assert b_ == 1
    uni = [jnp.concatenate([a, b], axis=1)[0] for a, b in zip(pc, pl_)]
    t_all = uni[0].shape[0]
    o2 = gated_delta_pallas(uni[0].reshape(t_all, DN_WIDTH), uni[1].reshape(t_all, DN_WIDTH), uni[2].reshape(t_all, DN_WIDTH),
                            jnp.transpose(uni[4], (1, 0, 2)), jnp.transpose(uni[3], (1, 0, 2)), tc)
    o = (o2[0] + o2[1]).reshape(1, t_all, DN_HEADS, DN_HEAD_DIM)
    o_ctx, o_lat = o[:, :tc], o[:, tc:]
    y_lat = gated_head_norm(o_lat, dn_lat[..., 3 * DN_WIDTH:4 * DN_WIDTH], norm_w)
    y_ctx = gated_head_norm(o_ctx, dn_ctx[..., 3 * DN_WIDTH:4 * DN_WIDTH], norm_w) if ctx_out else None
    return y_ctx, y_lat


def even_mixer(u_ctx, u_lat, ctx_out, w_in, hy_conv, hy_w1, hy_b1, hy_w2, hy_b2, hy_w3, hy_b3, hy_w4, hy_freq, hy_bias, dn_conv, dn_a_log, dn_dt_bias, dn_norm_w):
    hy_cols = 3 * HY_WIDTH
    n_main = 7 * HY_WIDTH
    filt = (hy_w1, hy_b1, hy_w2, hy_b2, hy_w3, hy_b3, hy_w4, hy_freq)
    w_tail = w_in[:, n_main:]

    def proj_in(u):
        lead = u.shape[:-1]
        a = u.reshape(1, -1, u.shape[-1]).astype(jnp.bfloat16)
        m = a.shape[1]
        tm = _pick(m, (512, 256))
        tn = 512
        main = pl.pallas_call(
            _mm_body,
            grid=(1, n_main // tn, m // tm),
            in_specs=[
                pl.BlockSpec((None, tm, D_MODEL), lambda ei, j, i: (ei, i, 0)),
                pl.BlockSpec((None, D_MODEL, tn), lambda ei, j, i: (ei, 0, j)),
            ],
            out_specs=pl.BlockSpec((None, tm, tn), lambda ei, j, i: (ei, i, j)),
            out_shape=jax.ShapeDtypeStruct((1, m, n_main), jnp.float32),
            scratch_shapes=[pltpu.VMEM((D_MODEL, tn), jnp.bfloat16)],
            compiler_params=pltpu.CompilerParams(
                dimension_semantics=("arbitrary", "arbitrary", "arbitrary"),
                vmem_limit_bytes=VMEM_LIMIT_BYTES),
            name="proj_in",
        )(a, w_in[None])
        tail = _bmm(a, w_tail[None])
        return jnp.concatenate([main, tail], axis=-1).reshape(lead + (w_in.shape[-1],))

    z_lat = proj_in(u_lat)
    z_ctx = proj_in(u_ctx)
    dn_ctx = z_ctx[..., hy_cols:]
    y_dn_ctx, y_dn_lat = deltanet(dn_ctx, z_lat[..., hy_cols:], ctx_out, dn_conv, dn_a_log, dn_dt_bias, dn_norm_w)
    y_lat = jnp.concatenate([hyena(z_lat[..., :hy_cols], hy_conv, filt, hy_bias), y_dn_lat], axis=-1)
    y_ctx = jnp.concatenate([hyena(z_ctx[..., :hy_cols], hy_conv, filt, hy_bias), y_dn_ctx], axis=-1) if ctx_out else None
    return y_ctx, y_lat


S5_GB = 8
S5_CB = S5_GB * S5_GROUP
S5_LB = S5_GB * S5_STATE
S5_NJ = S5_GROUPS // S5_GB
S5_LANES = S5_GROUPS * S5_STATE
S5_T_TILE = 1056
S5_SCAN_CHUNK = 256
S5_SCAN_LANES = 1024


def _s5_params(lam_re, lam_im, log_dt, b_re, b_im, c_re, c_im):
    f32 = jnp.float32
    lam = lax.complex(jnp.minimum(lam_re.astype(f32), S5_MAX_RE), lam_im.astype(f32))
    lam_bar = jnp.exp(lam * jnp.exp(log_dt.astype(f32))[..., None])
    b_bar = ((lam_bar - 1.0) / lam)[..., None] * lax.complex(b_re.astype(f32), b_im.astype(f32))
    eye = jnp.eye(S5_GB, dtype=f32)

    def w_in(part):
        part = part.reshape(2, S5_NJ, S5_GB, S5_STATE, S5_GROUP)
        return jnp.einsum('gh,djhpc->djgchp', eye, part).reshape(2, S5_NJ, S5_CB, S5_LB)

    def w_out(part):
        part = part.reshape(2, S5_NJ, S5_GB, S5_GROUP, S5_STATE)
        return jnp.einsum('gh,djhcp->djhpgc', eye, part).reshape(2, S5_NJ, S5_LB, S5_CB)

    wi_re, wi_im = w_in(jnp.real(b_bar)), w_in(jnp.imag(b_bar))
    w1 = jnp.stack([wi_re[0], wi_im[0], wi_re[1], wi_im[1]], axis=2).reshape(S5_NJ, S5_CB, 4 * S5_LB)
    wo_re, wo_im = w_out(c_re.astype(f32)), w_out(-c_im.astype(f32))
    w3 = jnp.stack([wo_re[0], wo_im[0], wo_re[1], wo_im[1]], axis=1)
    a = lam_bar.reshape(2, 1, S5_LANES)
    pows = [a]
    for _ in range(7):
        pows.append(pows[-1] * a)
    pows = jnp.concatenate(pows, axis=1)
    r = jnp.arange(8)[None, :, None]
    done = jnp.stack([r, 7 - r])
    done = done.reshape(2, 8, 1)
    tabs = []
    for sh in (1, 2, 4):
        tabs.append(jnp.where(done >= sh, pows[:, sh - 1:sh], 0.0))
    tabs.append(jnp.stack([pows[0], pows[1, ::-1]]))
    coef = jnp.stack([part for tab in tabs for part in (jnp.real(tab), jnp.imag(tab))], axis=1)
    return w1, w3, coef


def _s5_bu_body(u_ref, w_ref, o_ref):
    y = jnp.dot(u_ref[...].astype(jnp.bfloat16), w_ref[...].astype(jnp.bfloat16), preferred_element_type=jnp.float32)
    for q in range(4):
        o_ref[q] = y[:, q * S5_LB:(q + 1) * S5_LB]


def _s5_scan_body(bre_ref, bim_ref, coef_ref, xre_ref, xim_ref, cre_ref, cim_ref):

    @pl.when(pl.program_id(2) == 0)
    def _():
        cre_ref[...] = jnp.zeros_like(cre_ref)
        cim_ref[...] = jnp.zeros_like(cim_ref)

    n_groups = S5_SCAN_CHUNK // 8

    def run(rev):
        br = bre_ref[...]
        bi = bim_ref[...]
        for n, sh in enumerate((1, 2, 4)):
            are = coef_ref[2 * n][None]
            aim = coef_ref[2 * n + 1][None]
            shift = 8 - sh if rev else sh
            rr = pltpu.roll(br, shift, 1)
            ri = pltpu.roll(bi, shift, 1)
            br, bi = br + are * rr - aim * ri, bi + are * ri + aim * rr
        pre = coef_ref[6]
        pim = coef_ref[7]
        cr = cre_ref[...]
        ci = cim_ref[...]
        last = 0 if rev else 7
        for g in (range(n_groups - 1, -1, -1) if rev else range(n_groups)):
            xr = br[g] + pre * cr - pim * ci
            xi = bi[g] + pre * ci + pim * cr
            xre_ref[g] = xr
            xim_ref[g] = xi
            cr = xr[last:last + 1]
            ci = xi[last:last + 1]
        cre_ref[...] = cr
        cim_ref[...] = ci

    @pl.when(pl.program_id(0) == 0)
    def _():
        run(False)

    @pl.when(pl.program_id(0) == 1)
    def _():
        run(True)


def _s5_out_body(xre_ref, xim_ref, u_ref, w_ref, d_ref, o_ref):
    acc = None
    for d in range(2):
        for x_ref, q in ((xre_ref, 2 * d), (xim_ref, 2 * d + 1)):
            part = jnp.dot(x_ref[d].astype(jnp.bfloat16), w_ref[q].astype(jnp.bfloat16), preferred_element_type=jnp.float32)
            acc = part if acc is None else acc + part
    o_ref[...] = jax.nn.gelu(acc + d_ref[...] * u_ref[...])


def s5_pallas(u, n_ctx, lam_re, lam_im, log_dt, b_re, b_im, c_re, c_im, d_skip):
    t, d = u.shape
    w1, w3, coef = _s5_params(lam_re, lam_im, log_dt, b_re, b_im, c_re, c_im)
    nt = t // S5_T_TILE
    bu = pl.pallas_call(
        _s5_bu_body,
        grid=(S5_NJ, nt),
        in_specs=[pl.BlockSpec((S5_T_TILE, S5_CB), lambda j, i: (i, j)),
                  pl.BlockSpec((None, S5_CB, 4 * S5_LB), lambda j, i: (j, 0, 0))],
        out_specs=pl.BlockSpec((4, S5_T_TILE, S5_LB), lambda j, i: (0, i, j)),
        out_shape=jax.ShapeDtypeStruct((4, t, S5_LANES), jnp.float32),
        compiler_params=pltpu.CompilerParams(dimension_semantics=("arbitrary", "arbitrary"), vmem_limit_bytes=VMEM_LIMIT_BYTES),
        name="s5_bu",
    )(u, w1)

    assert n_ctx == S5_SCAN_CHUNK
    nch = t // S5_SCAN_CHUNK
    n_groups = S5_SCAN_CHUNK // 8
    bu4 = bu.reshape(4, t // 8, 8, S5_LANES)

    def tchunk(dd, cc):
        return jnp.where(dd == 1, _rev_chunk(cc, 1, nch), cc)

    def plane_spec(plane):
        return pl.BlockSpec((None, n_groups, 8, S5_SCAN_LANES), lambda dd, lb, cc: (2 * dd + plane, tchunk(dd, cc), 0, lb))

    coef_spec = pl.BlockSpec((None, 8, 8, S5_SCAN_LANES), lambda dd, lb, cc: (dd, 0, 0, lb))
    x_spec = pl.BlockSpec((None, n_groups, 8, S5_SCAN_LANES), lambda dd, lb, cc: (dd, tchunk(dd, cc), 0, lb))
    x_shape = jax.ShapeDtypeStruct((2, t // 8, 8, S5_LANES), jnp.float32)
    x_re, x_im = pl.pallas_call(
        _s5_scan_body,
        grid=(2, S5_LANES // S5_SCAN_LANES, nch),
        in_specs=[plane_spec(0), plane_spec(1), coef_spec],
        out_specs=[x_spec, x_spec],
        out_shape=[x_shape, x_shape],
        scratch_shapes=[pltpu.VMEM((1, S5_SCAN_LANES), jnp.float32), pltpu.VMEM((1, S5_SCAN_LANES), jnp.float32)],
        compiler_params=pltpu.CompilerParams(dimension_semantics=("arbitrary", "arbitrary", "arbitrary"), vmem_limit_bytes=VMEM_LIMIT_BYTES),
        name="s5_scan",
    )(bu4, bu4, coef)

    xs_spec = pl.BlockSpec((2, S5_T_TILE, S5_LB), lambda j, i: (0, i, j))
    return pl.pallas_call(
        _s5_out_body,
        grid=(S5_NJ, nt),
        in_specs=[xs_spec, xs_spec,
                  pl.BlockSpec((S5_T_TILE, S5_CB), lambda j, i: (i, j)),
                  pl.BlockSpec((None, 4, S5_LB, S5_CB), lambda j, i: (j, 0, 0, 0)),
                  pl.BlockSpec((1, S5_CB), lambda j, i: (0, j))],
        out_specs=pl.BlockSpec((S5_T_TILE, S5_CB), lambda j, i: (i, j)),
        out_shape=jax.ShapeDtypeStruct((t, d), jnp.float32),
        compiler_params=pltpu.CompilerParams(dimension_semantics=("arbitrary", "arbitrary"), vmem_limit_bytes=VMEM_LIMIT_BYTES),
        name="s5_out",
    )(x_re.reshape(2, t, S5_LANES), x_im.reshape(2, t, S5_LANES), u, w3, d_skip.astype(jnp.float32).reshape(1, d))


def s5_mixer(u_ctx, u_lat, ctx_out, lam_re, lam_im, log_dt, b_re, b_im, c_re, c_im, d_skip):
    b_, tc, d = u_ctx.shape
    assert b_ == 1
    u = jnp.concatenate([u_ctx[0], u_lat[0]], axis=0).astype(jnp.float32)
    f = s5_pallas(u, tc, lam_re, lam_im, log_dt, b_re, b_im, c_re, c_im, d_skip)
    f_lat = f[None, tc:]
    f_ctx = f[None, :tc] if ctx_out else None
    return f_ctx, f_lat


def glu_out(f, w):
    gv = _mm(f, w)
    return gv[..., :D_MODEL] * jax.nn.sigmoid(gv[..., D_MODEL:])


def expert_choice_ffn(h, w_router, w_in, w_out):
    b_, n, d = h.shape
    cap = EC_CAPACITY * n // N_EXPERTS
    aff = jax.nn.softmax(jnp.einsum('bnd,de->bne', h, w_router).astype(jnp.float32), axis=-1)
    gate, idx = lax.top_k(jnp.swapaxes(aff, 1, 2), cap)
    xs = jax.vmap(lambda hb, ib: hb[ib])(h, idx)
    gu = _bmm(xs[0], w_in)
    act = jax.nn.silu(gu[..., :EXPERT_FF]) * gu[..., EXPERT_FF:]
    out = (_bmm(act, w_out) * gate[0][..., None].astype(h.dtype))[None]
    return jax.vmap(lambda ib, ob: jnp.zeros((n, d), ob.dtype).at[ib.reshape(-1)].add(ob.reshape(-1, d)))(idx, out)


def kernel(x, c, ctx, c_ctx, ada_w, ada_b, ln_g, ln_b, ev_w_in, ev_w_out, hy_conv, hy_w1, hy_b1, hy_w2, hy_b2, hy_w3, hy_b3, hy_w4, hy_freq, hy_bias, dn_conv, dn_a_log, dn_dt_bias, dn_norm_w, s5_lam_re, s5_lam_im, s5_log_dt, s5_b_re, s5_b_im, s5_c_re, s5_c_im, s5_d, od_w_glu, moe_router, moe_w_in, moe_w_out):
    d = D_MODEL
    h_lat, h_ctx = x, ctx
    s_lat = jax.nn.silu(c)
    s_ctx = jax.nn.silu(c_ctx)[None]
    for l in range(DEPTH):
        last = l == DEPTH - 1
        col = (l // 2) % 2 == 1
        i = l // 2
        n_ctx_mod = 2 if last else 6
        mod_lat = jnp.split((s_lat @ ada_w[l] + ada_b[l])[:, None], 6, axis=-1)
        mod_ctx = jnp.split((s_ctx @ ada_w[l][:, :n_ctx_mod * d] + ada_b[l][:n_ctx_mod * d])[:, None], n_ctx_mod, axis=-1)
        u_lat = h_lat * (1.0 + mod_lat[1]) + mod_lat[0]
        u_ctx = h_ctx * (1.0 + mod_ctx[1]) + mod_ctx[0]
        if col:
            u_lat = to_col_major(u_lat)
        if l % 2 == 0:
            f_ctx, f_lat = even_mixer(u_ctx, u_lat, not last, ev_w_in[i], hy_conv[i], hy_w1[i], hy_b1[i], hy_w2[i], hy_b2[i], hy_w3[i], hy_b3[i], hy_w4[i], hy_freq[i], hy_bias[i], dn_conv[i], dn_a_log[i], dn_dt_bias[i], dn_norm_w[i])
            w_proj = ev_w_out[i]
            proj = _mm
        else:
            f_ctx, f_lat = s5_mixer(u_ctx, u_lat, not last, s5_lam_re[i], s5_lam_im[i], s5_log_dt[i], s5_b_re[i], s5_b_im[i], s5_c_re[i], s5_c_im[i], s5_d[i])
            w_proj = od_w_glu[i]
            proj = glu_out
        if col:
            f_lat = from_col_major(f_lat)
        h_lat = post_norm(h_lat, mod_lat[2] * proj(f_lat, w_proj), ln_g[l, 0], ln_b[l, 0])
        m_lat = expert_choice_ffn(h_lat * (1.0 + mod_lat[4]) + mod_lat[3], moe_router[l], moe_w_in[l], moe_w_out[l])
        h_lat = post_norm(h_lat, mod_lat[5] * m_lat, ln_g[l, 1], ln_b[l, 1])
        if not last:
            h_ctx = post_norm(h_ctx, mod_ctx[2] * proj(f_ctx, w_proj), ln_g[l, 0], ln_b[l, 0])
            m_ctx = expert_choice_ffn(h_ctx * (1.0 + mod_ctx[4]) + mod_ctx[3], moe_router[l], moe_w_in[l], moe_w_out[l])
            h_ctx = post_norm(h_ctx, mod_ctx[5] * m_ctx, ln_g[l, 1], ln_b[l, 1])
    return h_lat
```
